```python
import numpy as np
import jax, jax.numpy as jnp
from jax import lax

D_MODEL = 1024
BATCH = 4
SEQ = 8192
DEPTH = 2

SSD_HEADS = 8
SSD_HEAD_DIM = 64
SSD_INNER = SSD_HEADS * SSD_HEAD_DIM
SSD_GROUPS = 2
SSD_STATE = 128
SSD_CONV = 4
SSD_CHUNK = 128
SSD_XBC = SSD_INNER + 2 * SSD_GROUPS * SSD_STATE
RET_HEADS = 4
RET_QK_DIM = 64
RET_V_DIM = 128
RET_CHUNK = 128
MLA_HEADS = 8
MLA_Q_LORA = 256
MLA_KV_LORA = 128
MLA_NOPE = 64
MLA_ROPE = 32
MLA_V = 64
ATTN_BLOCK = 128
GLA_HEADS = 4
GLA_K_DIM = 64
GLA_V_DIM = 128
GLA_GATE_RANK = 16
GLA_GATE_NORM = 16.0
GLA_CHUNK = 64
N_BRANCHES = 4
BRANCH_WIDTH = 512
D_FF = 4 * D_MODEL
ROPE_THETA = 10000.0
RMS_EPS = 1e-6
MAX_POS_OFFSET = 4096
IN_WIDTH = (SSD_INNER + SSD_XBC + SSD_HEADS
            + RET_HEADS * (2 * RET_QK_DIM + 2 * RET_V_DIM)
            + MLA_Q_LORA + MLA_KV_LORA + MLA_ROPE
            + GLA_HEADS * (2 * GLA_K_DIM + 2 * GLA_V_DIM) + GLA_GATE_RANK
            + N_BRANCHES * D_MODEL)

kernel_name = "hybrid_ssd_ret_mla_gla_block"


def rms_norm(x, g, eps=RMS_EPS):
    xf = x.astype(jnp.float32)
    y = xf * lax.rsqrt(jnp.mean(xf * xf, axis=-1, keepdims=True) + eps)
    return (y * g.astype(jnp.float32)).astype(x.dtype)


def rope(x, positions):
    half = x.shape[-1] // 2
    inv_freq = ROPE_THETA ** (-jnp.arange(half, dtype=jnp.float32) / half)
    ang = positions.astype(jnp.float32)[:, :, None, None] * inv_freq
    cos, sin = jnp.cos(ang), jnp.sin(ang)
    xf = x.astype(jnp.float32)
    x1, x2 = xf[..., :half], xf[..., half:]
    return jnp.concatenate([x1 * cos - x2 * sin, x2 * cos + x1 * sin], axis=-1).astype(x.dtype)


def causal_depthwise_conv(x, w, b):
    k = w.shape[0]
    y = lax.conv_general_dilated(x, w[:, None, :], window_strides=(1,), padding=[(k - 1, 0)],
                                 dimension_numbers=('NWC', 'WIO', 'NWC'),
                                 feature_group_count=x.shape[-1])
    return y + b


def scalar_decay_chunked(q, k, v, log_g, chunk):
    b, l, h, n = q.shape
    p = v.shape[-1]
    nc = l // chunk
    qc = q.reshape(b, nc, chunk, h, n)
    kc = k.reshape(b, nc, chunk, h, n)
    vc = v.reshape(b, nc, chunk, h, p)
    a = log_g.astype(jnp.float32).reshape(b, nc, chunk, h).transpose(0, 3, 1, 2)
    cum = jnp.cumsum(a, axis=-1)
    seg = cum[..., :, None] - cum[..., None, :]
    mask = jnp.tril(jnp.ones((chunk, chunk), dtype=bool))
    decay = jnp.exp(jnp.where(mask, seg, -jnp.inf))
    scores = jnp.einsum('bcihn,bcjhn->bhcij', qc, kc) * decay
    y_diag = jnp.einsum('bhcij,bcjhp->bcihp', scores, vc)
    decay_to_end = jnp.exp(cum[..., -1:] - cum)
    states = jnp.einsum('bcjhn,bhcj,bcjhp->bchpn', kc, decay_to_end, vc)
    chunk_decay = jnp.exp(cum[..., -1])

    def step(carry, inp):
        dec, st = inp
        return dec[:, :, None, None] * carry + st, carry

    init = jnp.zeros((b, h, p, n), states.dtype)
    _, prev = lax.scan(step, init, (jnp.moveaxis(chunk_decay, 2, 0), jnp.moveaxis(states, 1, 0)))
    prev = jnp.moveaxis(prev, 0, 1)
    y_off = jnp.einsum('bcihn,bchpn,bhci->bcihp', qc, prev, jnp.exp(cum))
    return (y_diag + y_off).reshape(b, l, h, p).astype(v.dtype)


def gla_chunked(q, k, v, log_g, chunk):
    b, l, h, dk = q.shape
    dv = v.shape[-1]
    nc = l // chunk
    qc = q.reshape(b, nc, chunk, h, dk)
    kc = k.reshape(b, nc, chunk, h, dk)
    vc = v.reshape(b, nc, chunk, h, dv)
    cum = jnp.cumsum(log_g.astype(jnp.float32).reshape(b, nc, chunk, h, dk), axis=2)
    total = cum[:, :, -1:]
    q_in = qc * jnp.exp(cum)
    k_in = kc * jnp.exp(-cum)
    mask = jnp.tril(jnp.ones((chunk, chunk), dtype=bool))
    scores = jnp.where(mask, jnp.einsum('bcihd,bcjhd->bchij', q_in, k_in), 0.0)
    o_intra = jnp.einsum('bchij,bcjhv->bcihv', scores, vc)
    k_st = kc * jnp.exp(total - cum)
    states = jnp.einsum('bcjhd,bcjhv->bchdv', k_st, vc)
    chunk_decay = jnp.exp(total[:, :, 0])

    def step(carry, inp):
        dec, st = inp
        return dec[..., None] * carry + st, carry

    init = jnp.zeros((b, h, dk, dv), states.dtype)
    _, prev = lax.scan(step, init, (jnp.moveaxis(chunk_decay, 1, 0), jnp.moveaxis(states, 1, 0)))
    prev = jnp.moveaxis(prev, 0, 1)
    o_inter = jnp.einsum('bcihd,bchdv->bcihv', q_in, prev)
    return (o_intra + o_inter).reshape(b, l, h, dv).astype(v.dtype)


def causal_attention_blocked(q, k, v, scale):
    b, l, h, d = q.shape
    dv = v.shape[-1]
    nb = l // ATTN_BLOCK
    qb = q.reshape(b, nb, ATTN_BLOCK, h, d).transpose(1, 0, 2, 3, 4)
    starts = jnp.arange(nb, dtype=jnp.int32) * ATTN_BLOCK
    kpos = jnp.arange(l, dtype=jnp.int32)

    def one_block(args):
        qi, s0 = args
        s = jnp.einsum('bqhd,bkhd->bhqk', qi, k, preferred_element_type=jnp.float32) * scale
        qpos = s0 + jnp.arange(ATTN_BLOCK, dtype=jnp.int32)
        s = jnp.where(kpos[None, :] <= qpos[:, None], s, -jnp.inf)
        pr = jax.nn.softmax(s, axis=-1).astype(v.dtype)
        return jnp.einsum('bhqk,bkhd->bqhd', pr, v)

    o = lax.map(one_block, (qb, starts))
    return o.transpose(1, 0, 2, 3, 4).reshape(b, l, h, dv)


def ssd_mixer(z, xbc, dt_raw, conv_w, conv_b, dt_bias, a_log, d_skip, norm_g):
    b, l, _ = z.shape
    xbc = jax.nn.silu(causal_depthwise_conv(xbc, conv_w, conv_b))
    xs, bm, cm = jnp.split(xbc, [SSD_INNER, SSD_INNER + SSD_GROUPS * SSD_STATE], axis=-1)
    xs = xs.reshape(b, l, SSD_HEADS, SSD_HEAD_DIM)
    rep = SSD_HEADS // SSD_GROUPS
    bm = jnp.repeat(bm.reshape(b, l, SSD_GROUPS, SSD_STATE), rep, axis=2)
    cm = jnp.repeat(cm.reshape(b, l, SSD_GROUPS, SSD_STATE), rep, axis=2)
    dt = jax.nn.softplus(dt_raw.astype(jnp.float32) + dt_bias.astype(jnp.float32))
    a = -jnp.exp(a_log.astype(jnp.float32))
    y = scalar_decay_chunked(cm, bm, xs * dt[..., None].astype(xs.dtype), dt * a, SSD_CHUNK)
    y = y + d_skip[:, None] * xs
    y = y.reshape(b, l, SSD_INNER) * jax.nn.silu(z)
    y = rms_norm(y.reshape(b, l, SSD_GROUPS, SSD_INNER // SSD_GROUPS),
                 norm_g.reshape(SSD_GROUPS, SSD_INNER // SSD_GROUPS))
    return y.reshape(b, l, SSD_INNER)


def retention_mixer(q, k, v, g, positions, norm_g):
    b, l, _ = q.shape
    q = rope(q.reshape(b, l, RET_HEADS, RET_QK_DIM), positions)
    k = rope(k.reshape(b, l, RET_HEADS, RET_QK_DIM), positions) * (RET_QK_DIM ** -0.5)
    v = v.reshape(b, l, RET_HEADS, RET_V_DIM)
    log_gamma = jnp.log1p(-jnp.exp2(-5.0 - jnp.arange(RET_HEADS, dtype=jnp.float32)))
    log_g = jnp.broadcast_to(log_gamma, (b, l, RET_HEADS))
    y = scalar_decay_chunked(q, k, v, log_g, RET_CHUNK)
    y = rms_norm(y, norm_g.reshape(RET_HEADS, RET_V_DIM)).reshape(b, l, RET_HEADS * RET_V_DIM)
    return y * jax.nn.silu(g)


def mla_mixer(c_q, c_kv, k_rope, positions, q_norm, w_uq, kv_norm, w_ukv):
    b, l, _ = c_q.shape
    q = (rms_norm(c_q, q_norm) @ w_uq).reshape(b, l, MLA_HEADS, MLA_NOPE + MLA_ROPE)
    q_nope, q_pe = q[..., :MLA_NOPE], rope(q[..., MLA_NOPE:], positions)
    kv = (rms_norm(c_kv, kv_norm) @ w_ukv).reshape(b, l, MLA_HEADS, MLA_NOPE + MLA_V)
    k_nope, v = kv[..., :MLA_NOPE], kv[..., MLA_NOPE:]
    k_pe = rope(k_rope[:, :, None, :], positions)
    q = jnp.concatenate([q_nope, q_pe], axis=-1)
    k = jnp.concatenate([k_nope, jnp.broadcast_to(k_pe, (b, l, MLA_HEADS, MLA_ROPE))], axis=-1)
    o = causal_attention_blocked(q, k, v, (MLA_NOPE + MLA_ROPE) ** -0.5)
    return o.reshape(b, l, MLA_HEADS * MLA_V)


def gla_mixer(q, k, v, gk_low, g, w_gk2, b_gk, norm_g):
    b, l, _ = q.shape
    q = q.reshape(b, l, GLA_HEADS, GLA_K_DIM) * (GLA_K_DIM ** -0.5)
    k = k.reshape(b, l, GLA_HEADS, GLA_K_DIM)
    v = v.reshape(b, l, GLA_HEADS, GLA_V_DIM)
    log_g = jax.nn.log_sigmoid((gk_low @ w_gk2 + b_gk).astype(jnp.float32)) / GLA_GATE_NORM
    o = gla_chunked(q, k, v, log_g.reshape(b, l, GLA_HEADS, GLA_K_DIM), GLA_CHUNK)
    o = rms_norm(o, norm_g.reshape(GLA_HEADS, GLA_V_DIM)).reshape(b, l, GLA_HEADS * GLA_V_DIM)
    return o * jax.nn.silu(g)


def hybrid_layer(x, positions, w_in, b_gate, ssd_conv_w, ssd_conv_b, ssd_dt_bias, ssd_a_log, ssd_d,
                 ssd_norm, ret_norm, mla_q_norm, mla_w_uq, mla_kv_norm, mla_w_ukv, gla_w_gk2, gla_b_gk,
                 gla_norm, w_branch, w_out, norm_pre_mix, norm_post_mix, norm_pre_mlp, norm_post_mlp,
                 w_mlp_in, w_mlp_out):
    b, l, _ = x.shape
    h = rms_norm(x, norm_pre_mix)
    sizes = (SSD_INNER, SSD_XBC, SSD_HEADS,
             RET_HEADS * RET_QK_DIM, RET_HEADS * RET_QK_DIM, RET_HEADS * RET_V_DIM, RET_HEADS * RET_V_DIM,
             MLA_Q_LORA, MLA_KV_LORA, MLA_ROPE,
             GLA_HEADS * GLA_K_DIM, GLA_HEADS * GLA_K_DIM, GLA_HEADS * GLA_V_DIM, GLA_GATE_RANK,
             GLA_HEADS * GLA_V_DIM, N_BRANCHES * D_MODEL)
    idx = [int(s) for s in np.cumsum(sizes)[:-1]]
    (z, xbc, dt_raw, rq, rk, rv, rg, cq, ckv, kr,
     gq, gk, gv, ggk, gg, gate_logits) = jnp.split(h @ w_in, idx, axis=-1)

    y_ssd = ssd_mixer(z, xbc, dt_raw, ssd_conv_w, ssd_conv_b, ssd_dt_bias, ssd_a_log, ssd_d, ssd_norm)
    y_ret = retention_mixer(rq, rk, rv, rg, positions, ret_norm)
    y_mla = mla_mixer(cq, ckv, kr, positions, mla_q_norm, mla_w_uq, mla_kv_norm, mla_w_ukv)
    y_gla = gla_mixer(gq, gk, gv, ggk, gg, gla_w_gk2, gla_b_gk, gla_norm)

    ys = jnp.stack([y_ssd, y_ret, y_mla, y_gla], axis=2)
    branches = jnp.einsum('blnw,nwd->blnd', ys, w_branch)
    gates = jax.nn.sigmoid((gate_logits + b_gate).astype(jnp.float32)).astype(x.dtype)
    merged = jnp.sum(gates.reshape(b, l, N_BRANCHES, D_MODEL) * branches, axis=2)
    x = x + rms_norm(merged @ w_out, norm_post_mix)

    u = jnp.square(jax.nn.relu(rms_norm(x, norm_pre_mlp) @ w_mlp_in))
    return x + rms_norm(u @ w_mlp_out, norm_post_mlp)


def setup_inputs(seed: int = 0) -> dict:
    key = jax.random.key(seed)
    ks = jax.random.split(key, 32)
    f32 = jnp.float32
    L = DEPTH

    def nrm(k, shape, scale):
        return jax.random.normal(k, shape, f32) * scale

    def gain(k, shape):
        return 1.0 + 0.02 * jax.random.normal(k, shape, f32)

    x = nrm(ks[0], (BATCH, SEQ, D_MODEL), 1.0)
    offsets = jax.random.randint(ks[1], (BATCH, 1), 0, MAX_POS_OFFSET, dtype=jnp.int32)
    positions = offsets + jnp.arange(SEQ, dtype=jnp.int32)[None, :]
    dt = jnp.exp(jax.random.uniform(ks[2], (L, SSD_HEADS), f32) * (np.log(0.1) - np.log(0.001))
                 + np.log(0.001))
    ssd_dt_bias = dt + jnp.log(-jnp.expm1(-dt))
    ssd_a_log = jnp.log(jax.random.uniform(ks[3], (L, SSD_HEADS), f32, minval=1.0, maxval=16.0))
    return {
        "x": x,
        "positions": positions,
        "w_in": nrm(ks[4], (L, D_MODEL, IN_WIDTH), D_MODEL ** -0.5),
        "b_gate": nrm(ks[5], (L, N_BRANCHES * D_MODEL), 0.01),
        "ssd_conv_w": nrm(ks[6], (L, SSD_CONV, SSD_XBC), SSD_CONV ** -0.5),
        "ssd_conv_b": nrm(ks[7], (L, SSD_XBC), 0.01),
        "ssd_dt_bias": ssd_dt_bias,
        "ssd_a_log": ssd_a_log,
        "ssd_d": 1.0 + 0.1 * jax.random.normal(ks[8], (L, SSD_HEADS), f32),
        "ssd_norm": gain(ks[9], (L, SSD_INNER)),
        "ret_norm": gain(ks[10], (L, RET_HEADS * RET_V_DIM)),
        "mla_q_norm": gain(ks[11], (L, MLA_Q_LORA)),
        "mla_w_uq": nrm(ks[12], (L, MLA_Q_LORA, MLA_HEADS * (MLA_NOPE + MLA_ROPE)), MLA_Q_LORA ** -0.5),
        "mla_kv_norm": gain(ks[13], (L, MLA_KV_LORA)),
        "mla_w_ukv": nrm(ks[14], (L, MLA_KV_LORA, MLA_HEADS * (MLA_NOPE + MLA_V)), MLA_KV_LORA ** -0.5),
        "gla_w_gk2": nrm(ks[15], (L, GLA_GATE_RANK, GLA_HEADS * GLA_K_DIM), GLA_GATE_RANK ** -0.5),
        "gla_b_gk": nrm(ks[16], (L, GLA_HEADS * GLA_K_DIM), 0.01),
        "gla_norm": gain(ks[17], (L, GLA_HEADS * GLA_V_DIM)),
        "w_branch": nrm(ks[18], (L, N_BRANCHES, BRANCH_WIDTH, D_MODEL), BRANCH_WIDTH ** -0.5),
        "w_out": nrm(ks[19], (L, D_MODEL, D_MODEL), D_MODEL ** -0.5),
        "norm_pre_mix": gain(ks[20], (L, D_MODEL)),
        "norm_post_mix": gain(ks[21], (L, D_MODEL)),
        "norm_pre_mlp": gain(ks[22], (L, D_MODEL)),
        "norm_post_mlp": gain(ks[23], (L, D_MODEL)),
        "w_mlp_in": nrm(ks[24], (L, D_MODEL, D_FF), D_MODEL ** -0.5),
        "w_mlp_out": nrm(ks[25], (L, D_FF, D_MODEL), D_FF ** -0.5),
    }


def reference(x, positions, w_in, b_gate, ssd_conv_w, ssd_conv_b, ssd_dt_bias, ssd_a_log, ssd_d,
              ssd_norm, ret_norm, mla_q_norm, mla_w_uq, mla_kv_norm, mla_w_ukv, gla_w_gk2, gla_b_gk,
              gla_norm, w_branch, w_out, norm_pre_mix, norm_post_mix, norm_pre_mlp, norm_post_mlp,
              w_mlp_in, w_mlp_out):
    for i in range(DEPTH):
        x = hybrid_layer(x, positions, w_in[i], b_gate[i], ssd_conv_w[i], ssd_conv_b[i], ssd_dt_bias[i],
                         ssd_a_log[i], ssd_d[i], ssd_norm[i], ret_norm[i], mla_q_norm[i], mla_w_uq[i],
                         mla_kv_norm[i], mla_w_ukv[i], gla_w_gk2[i], gla_b_gk[i], gla_norm[i], w_branch[i],
                         w_out[i], norm_pre_mix[i], norm_post_mix[i], norm_pre_mlp[i], norm_post_mlp[i],
                         w_mlp_in[i], w_mlp_out[i])
    return x
```

```python
import functools

import numpy as np
import jax
import jax.numpy as jnp
from jax import lax
from jax.experimental import pallas as pl
from jax.experimental.pallas import tpu as pltpu

F32 = jnp.float32
BF16 = jnp.bfloat16

D_MODEL = 1024
SSD_HEADS = 8
SSD_HEAD_DIM = 64
SSD_INNER = 512
SSD_GROUPS = 2
SSD_STATE = 128
SSD_CONV = 4
SSD_CHUNK = 128
SSD_XBC = 1024
RET_HEADS = 4
RET_QK_DIM = 64
RET_V_DIM = 128
RET_CHUNK = 128
MLA_HEADS = 8
MLA_Q_LORA = 256
MLA_KV_LORA = 128
MLA_NOPE = 64
MLA_ROPE = 32
MLA_V = 64
GLA_HEADS = 4
GLA_K_DIM = 64
GLA_V_DIM = 128
GLA_GATE_RANK = 16
GLA_GATE_NORM = 16.0
GLA_CHUNK = 64
N_BRANCHES = 4
BRANCH_WIDTH = 512
D_FF = 4 * D_MODEL
ROPE_THETA = 10000.0
RMS_EPS = 1e-6

LANES = 128
MLA_HEAD_PAD = 128
VMEM_LIMIT = 56 * 1024 * 1024

TOKEN_TILE = 512
ATTN_BLOCK_Q = 512
ATTN_BLOCK_K = 512
GLA_STEP = 128

_SIZES = (SSD_INNER, SSD_XBC, SSD_HEADS,
          RET_HEADS * RET_QK_DIM, RET_HEADS * RET_QK_DIM, RET_HEADS * RET_V_DIM, RET_HEADS * RET_V_DIM,
          MLA_Q_LORA, MLA_KV_LORA, MLA_ROPE,
          GLA_HEADS * GLA_K_DIM, GLA_HEADS * GLA_K_DIM, GLA_HEADS * GLA_V_DIM, GLA_GATE_RANK,
          GLA_HEADS * GLA_V_DIM, N_BRANCHES * D_MODEL)
_OFFS = tuple(int(v) for v in np.concatenate([[0], np.cumsum(_SIZES)]))
(_O_Z, _O_XBC, _O_DT, _O_RQ, _O_RK, _O_RV, _O_RG, _O_CQ, _O_CKV, _O_KR,
 _O_GQ, _O_GK, _O_GV, _O_GGK, _O_GG, _O_GATE, _O_END) = _OFFS

SMALL_DT_LANE = 0
SMALL_GGK_LANE = 8
MLA_PE_LANE = MLA_NOPE


def _dot(a, b, precision=None):
    return jnp.dot(a, b, preferred_element_type=F32, precision=precision)


def _dot_nt(a, b):
    return lax.dot_general(a, b, (((1,), (1,)), ((), ())), preferred_element_type=F32)


def _dot_tn(a, b):
    return lax.dot_general(a, b, (((0,), (0,)), ((), ())), preferred_element_type=F32)


def _rms(x, g):
    return x * lax.rsqrt(jnp.mean(x * x, axis=-1, keepdims=True) + RMS_EPS) * g


def _silu(x):
    return x * jax.nn.sigmoid(x)


def _softplus(x):
    return jnp.maximum(x, 0.0) + jnp.log1p(jnp.exp(-jnp.abs(x)))


def _params(*sem):
    return pltpu.CompilerParams(dimension_semantics=sem, vmem_limit_bytes=VMEM_LIMIT)


def _const_spec(shape):
    nd = len(shape)
    return pl.BlockSpec(shape, lambda *_: (0,) * nd, pipeline_mode=pl.Buffered(1))


def _tril(n):
    r = lax.broadcasted_iota(jnp.int32, (n, n), 0)
    c = lax.broadcasted_iota(jnp.int32, (n, n), 1)
    return r >= c


def _rope_kernel(pos_ref, c_ref, cr_ref, sr_ref, cm_ref, sm_ref):
    pos = pos_ref[...]
    c = c_ref[...]
    ang_r = pos * c[0:1, :]
    cr_ref[...] = jnp.cos(ang_r)
    sr_ref[...] = jnp.sin(ang_r) * c[1:2, :]
    ang_m = pos * c[2:3, :]
    cm_ref[...] = jnp.cos(ang_m)
    sm_ref[...] = jnp.sin(ang_m) * c[3:4, :]


def _rope_consts():
    lane = np.arange(LANES)
    half_r = RET_QK_DIM // 2
    half_m = MLA_ROPE // 2
    inv_r = ROPE_THETA ** (-jnp.arange(half_r, dtype=F32) / half_r)
    inv_m = ROPE_THETA ** (-jnp.arange(half_m, dtype=F32) / half_m)
    freq_r = inv_r[lane % half_r]
    sign_r = jnp.asarray(np.where((lane % RET_QK_DIM) < half_r, -1.0, 1.0), F32)
    in_pe = (lane >= MLA_PE_LANE) & (lane < MLA_PE_LANE + MLA_ROPE)
    freq_m = jnp.where(in_pe, inv_m[(lane - MLA_PE_LANE) % half_m], 0.0)
    sign_m = jnp.asarray(np.where(in_pe, np.where((lane - MLA_PE_LANE) < half_m, -1.0, 1.0), 0.0), F32)
    z = jnp.zeros((LANES,), F32)
    return jnp.stack([freq_r, sign_r, freq_m, sign_m, z, z, z, z]).astype(F32)


def _rope_tables(positions):
    b, l = positions.shape
    t = b * l
    tb = min(1024, t)
    pos = positions.astype(F32).reshape(t, 1)
    tab = jax.ShapeDtypeStruct((t, LANES), F32)
    spec = pl.BlockSpec((tb, LANES), lambda i: (i, 0))
    return pl.pallas_call(
        _rope_kernel,
        out_shape=(tab, tab, tab, tab),
        grid=(t // tb,),
        in_specs=[pl.BlockSpec((tb, 1), lambda i: (i, 0)), _const_spec((8, LANES))],
        out_specs=(spec, spec, spec, spec),
        compiler_params=_params("arbitrary"),
        name="rope_tables",
    )(pos, _rope_consts())


def _inproj_kernel(x_ref, g_ref, wssd_ref, wret_ref, wmla_ref, wgla_ref, wsm_ref,
                   ossd_ref, oret_ref, omla_ref, ogla_ref, osm_ref):
    h = _rms(x_ref[...], g_ref[...]).astype(BF16)
    ossd_ref[...] = _dot(h, wssd_ref[...]).astype(BF16)
    oret_ref[...] = _dot(h, wret_ref[...]).astype(BF16)
    omla_ref[...] = _dot(h, wmla_ref[...]).astype(BF16)
    ogla_ref[...] = _dot(h, wgla_ref[...]).astype(BF16)
    osm_ref[...] = _dot(h, wsm_ref[...])


def _inproj(x2, g, wssd, wret, wmla, wgla, wsm):
    t = x2.shape[0]
    tm = min(TOKEN_TILE, t)
    ws = (wssd, wret, wmla, wgla, wsm)
    outs = tuple(jax.ShapeDtypeStruct((t, w.shape[1]), BF16) for w in ws[:4]) + (
        jax.ShapeDtypeStruct((t, LANES), F32),)
    return pl.pallas_call(
        _inproj_kernel,
        out_shape=outs,
        grid=(t // tm,),
        in_specs=[pl.BlockSpec((tm, D_MODEL), lambda i: (i, 0)), _const_spec((1, D_MODEL))]
                 + [_const_spec(w.shape) for w in ws],
        out_specs=tuple(pl.BlockSpec((tm, o.shape[1]), lambda i: (i, 0)) for o in outs),
        compiler_params=_params("arbitrary"),
        name="in_proj",
    )(x2, g, *ws)


def _ssd_kernel(blk_ref, sm_ref, cw_ref, cb_ref, dtb_ref, alog_ref, exp_ref, dsk_ref, ng_ref,
                y_ref, conv_ref, state_ref):
    q = SSD_CHUNK
    tail = 8

    @pl.when(pl.program_id(1) == 0)
    def _():
        conv_ref[0:tail, :] = jnp.zeros((tail, SSD_XBC), F32)
        state_ref[...] = jnp.zeros_like(state_ref)

    z = blk_ref[:, 0:SSD_INNER].astype(F32)
    xbc = blk_ref[:, SSD_INNER:SSD_INNER + SSD_XBC].astype(F32)
    conv_ref[tail:tail + q, :] = xbc
    acc = cb_ref[...] + cw_ref[SSD_CONV - 1:SSD_CONV, :] * xbc
    for j in range(SSD_CONV - 1):
        off = tail - (SSD_CONV - 1) + j
        acc = acc + cw_ref[j:j + 1, :] * conv_ref[off:off + q, :]
    conv_ref[0:tail, :] = xbc[q - tail:q, :]
    xbc = _silu(acc)
    xs = xbc[:, 0:SSD_INNER]
    gn = SSD_GROUPS * SSD_STATE
    bm = xbc[:, SSD_INNER:SSD_INNER + gn].astype(BF16)
    cm = xbc[:, SSD_INNER + gn:SSD_INNER + 2 * gn].astype(BF16)

    hi = lax.Precision.HIGHEST
    dt = _softplus(sm_ref[...] + dtb_ref[...])
    la = dt * (-jnp.exp(alog_ref[...]))
    mask = _tril(q)
    cum = _dot(mask.astype(F32), la, hi)
    cum_t = cum.T
    expand = exp_ref[...]
    cum_ch = _dot(cum, expand, hi)
    dt_ch = _dot(dt, expand, hi)
    xdt = xs * dt_ch
    xdt_b = xdt.astype(BF16)
    last = cum_ch[q - 1:q, :]
    xdte_b = (xdt * jnp.exp(last - cum_ch)).astype(BF16)
    chunk_decay = jnp.exp(last)

    hpg = SSD_HEADS // SSD_GROUPS
    gw = hpg * SSD_HEAD_DIM
    lane_lo = lax.broadcasted_iota(jnp.int32, (q, LANES), 1) < SSD_HEAD_DIM
    ydiag, yoff = [], []
    for g in range(SSD_GROUPS):
        cg = cm[:, g * SSD_STATE:(g + 1) * SSD_STATE]
        bg = bm[:, g * SSD_STATE:(g + 1) * SSD_STATE]
        s = _dot_nt(cg, bg)
        st = state_ref[g]
        yoff.append(_dot(cg, st.astype(BF16)))
        for j in range(hpg // 2):
            xpair = xdt_b[:, g * gw + j * LANES:g * gw + (j + 1) * LANES]
            ys = []
            for e in range(2):
                h = g * hpg + 2 * j + e
                seg = cum[:, h:h + 1] - cum_t[h:h + 1, :]
                a = (s * jnp.exp(jnp.where(mask, seg, -jnp.inf))).astype(BF16)
                ys.append(_dot(a, xpair))
            ydiag.append(jnp.where(lane_lo, ys[0], ys[1]))
        state_ref[g] = st * chunk_decay[:, g * gw:(g + 1) * gw] + _dot_tn(bg, xdte_b[:, g * gw:(g + 1) * gw])
    y = jnp.concatenate(ydiag, axis=1) + jnp.concatenate(yoff, axis=1) * jnp.exp(cum_ch)
    y = (y + dsk_ref[...] * xs) * _silu(z)
    outs = []
    for g in range(SSD_GROUPS):
        outs.append(_rms(y[:, g * gw:(g + 1) * gw], ng_ref[:, g * gw:(g + 1) * gw]))
    y_ref[...] = jnp.concatenate(outs, axis=1).astype(BF16)


def _ssd(o_ssd, o_small, conv_w, conv_b, dt_bias, a_log, d_skip, norm_g, b, l):
    q = SSD_CHUNK
    nc = l // q
    pad = LANES - SSD_HEADS
    dtb = jnp.pad(dt_bias, (0, pad)).reshape(1, LANES)
    alog = jnp.pad(a_log, (0, pad)).reshape(1, LANES)
    head_of = np.arange(SSD_INNER) // SSD_HEAD_DIM
    expand = jnp.asarray(np.arange(LANES)[:, None] == head_of[None, :], F32)
    dsk = jnp.repeat(d_skip, SSD_HEAD_DIM).reshape(1, SSD_INNER)
    row = lambda bi, ci: (bi * nc + ci, 0)
    return pl.pallas_call(
        _ssd_kernel,
        out_shape=jax.ShapeDtypeStruct((b * l, SSD_INNER), BF16),
        grid=(b, nc),
        in_specs=[pl.BlockSpec((q, SSD_INNER + SSD_XBC), row), pl.BlockSpec((q, LANES), row),
                  _const_spec((SSD_CONV, SSD_XBC)), _const_spec((1, SSD_XBC)),
                  _const_spec((1, LANES)), _const_spec((1, LANES)), _const_spec((LANES, SSD_INNER)),
                  _const_spec((1, SSD_INNER)), _const_spec((1, SSD_INNER))],
        out_specs=pl.BlockSpec((q, SSD_INNER), row),
        scratch_shapes=[pltpu.VMEM((8 + q, SSD_XBC), F32),
                        pltpu.VMEM((SSD_GROUPS, SSD_STATE, SSD_INNER // SSD_GROUPS), F32)],
        compiler_params=_params("arbitrary", "arbitrary"),
        name="ssd_mixer",
    )(o_ssd, o_small, conv_w, conv_b.reshape(1, -1), dtb, alog, expand, dsk, norm_g.reshape(1, -1))


def _ret_kernel(blk_ref, c_ref, s_ref, dm_ref, qs_ref, ks_ref, cd_ref, bd_ref, ng_ref, y_ref, state_ref):
    @pl.when(pl.program_id(1) == 0)
    def _():
        state_ref[...] = jnp.zeros_like(state_ref)

    w = RET_HEADS * RET_QK_DIM
    vw = RET_HEADS * RET_V_DIM
    ct = jnp.concatenate([c_ref[...]] * (w // LANES), axis=1)
    st = jnp.concatenate([s_ref[...]] * (w // LANES), axis=1)
    q = blk_ref[:, 0:w].astype(F32) * ct + blk_ref[:, 2 * w:3 * w].astype(F32) * st
    k = (blk_ref[:, w:2 * w].astype(F32) * ct + blk_ref[:, 3 * w:4 * w].astype(F32) * st) * (RET_QK_DIM ** -0.5)
    v = blk_ref[:, 4 * w:4 * w + vw]
    g = blk_ref[:, 4 * w + vw:4 * w + 2 * vw].astype(F32)
    k_b = k.astype(BF16)
    lane = lax.broadcasted_iota(jnp.int32, (1, w), 1)
    state = state_ref[...]
    y_off = _dot((q * qs_ref[...]).astype(BF16), state.astype(BF16))
    ys = []
    for h in range(RET_HEADS):
        in_head = (lane >= h * RET_QK_DIM) & (lane < (h + 1) * RET_QK_DIM)
        qh = jnp.where(in_head, q, 0.0).astype(BF16)
        sc = (_dot_nt(qh, k_b) * dm_ref[h]).astype(BF16)
        vh = v[:, h * RET_V_DIM:(h + 1) * RET_V_DIM]
        yh = _dot(sc, vh) + y_off[:, h * RET_V_DIM:(h + 1) * RET_V_DIM]
        ys.append(_rms(yh, ng_ref[:, h * RET_V_DIM:(h + 1) * RET_V_DIM]))
    new_state = state * cd_ref[...] + _dot_tn((k * ks_ref[...]).astype(BF16), v)
    state_ref[...] = new_state * bd_ref[...]
    y_ref[...] = (jnp.concatenate(ys, axis=1) * _silu(g)).astype(BF16)


def _ret_consts():
    q = RET_CHUNK
    lg = np.log1p(-np.exp2(-5.0 - np.arange(RET_HEADS, dtype=np.float64)))
    i = np.arange(q)
    dm = np.where(i[:, None] >= i[None, :], np.exp(lg[:, None, None] * (i[:, None] - i[None, :])[None]), 0.0)
    qs = np.repeat(np.exp(lg[None, :] * (i[:, None] + 1)), RET_QK_DIM, axis=1)
    ks = np.repeat(np.exp(lg[None, :] * (q - 1 - i[:, None])), RET_QK_DIM, axis=1)
    cd = np.repeat(np.exp(lg * q), RET_V_DIM)[None, :]
    bd = (np.arange(RET_HEADS * RET_QK_DIM)[:, None] // RET_QK_DIM
          == np.arange(RET_HEADS * RET_V_DIM)[None, :] // RET_V_DIM)
    return tuple(jnp.asarray(a, F32) for a in (dm, qs, ks, cd, bd))


def _ret(o_ret, c_tab, s_tab, norm_g, b, l):
    q = RET_CHUNK
    nc = l // q
    w = RET_HEADS * RET_QK_DIM
    vw = RET_HEADS * RET_V_DIM
    dm, qs, ks, cd, bd = _ret_consts()
    row = lambda bi, ci: (bi * nc + ci, 0)
    return pl.pallas_call(
        _ret_kernel,
        out_shape=jax.ShapeDtypeStruct((b * l, vw), BF16),
        grid=(b, nc),
        in_specs=[pl.BlockSpec((q, 4 * w + 2 * vw), row), pl.BlockSpec((q, LANES), row),
                  pl.BlockSpec((q, LANES), row),
                  _const_spec(dm.shape), _const_spec(qs.shape), _const_spec(ks.shape),
                  _const_spec(cd.shape), _const_spec(bd.shape), _const_spec((1, vw))],
        out_specs=pl.BlockSpec((q, vw), row),
        scratch_shapes=[pltpu.VMEM((w, vw), F32)],
        compiler_params=_params("arbitrary", "arbitrary"),
        name="ret_mixer",
    )(o_ret, c_tab, s_tab, dm, qs, ks, cd, bd, norm_g.reshape(1, -1))


def _gla_kernel(blk_ref, sm_ref, w2_ref, bgk_ref, bd_ref, ng_ref, y_ref, state_ref):
    @pl.when(pl.program_id(1) == 0)
    def _():
        state_ref[...] = jnp.zeros_like(state_ref)

    c = GLA_CHUNK
    kw = GLA_HEADS * GLA_K_DIM
    vw = GLA_HEADS * GLA_V_DIM
    hi = lax.Precision.HIGHEST
    logits = _dot(sm_ref[...], w2_ref[...], hi) + bgk_ref[...]
    log_g = -_softplus(-logits) * (1.0 / GLA_GATE_NORM)
    mask = _tril(c)
    tri = mask.astype(F32)
    lane = lax.broadcasted_iota(jnp.int32, (1, kw), 1)
    for s in range(GLA_STEP // c):
        r0 = s * c
        qv = blk_ref[r0:r0 + c, 0:kw].astype(F32) * (GLA_K_DIM ** -0.5)
        kv = blk_ref[r0:r0 + c, kw:2 * kw].astype(F32)
        v = blk_ref[r0:r0 + c, 2 * kw:2 * kw + vw]
        g = blk_ref[r0:r0 + c, 2 * kw + vw:2 * kw + 2 * vw].astype(F32)
        cum = _dot(tri, log_g[r0:r0 + c, :], hi)
        total = cum[c - 1:c, :]
        q_in = qv * jnp.exp(cum)
        k_in = (kv * jnp.exp(-cum)).astype(BF16)
        k_st = (kv * jnp.exp(total - cum)).astype(BF16)
        state = state_ref[...]
        o_inter = _dot_nt(q_in.astype(BF16), state.astype(BF16))
        outs = []
        for h in range(GLA_HEADS):
            in_head = (lane >= h * GLA_K_DIM) & (lane < (h + 1) * GLA_K_DIM)
            qh = jnp.where(in_head, q_in, 0.0).astype(BF16)
            sc = jnp.where(mask, _dot_nt(qh, k_in), 0.0).astype(BF16)
            oh = _dot(sc, v[:, h * GLA_V_DIM:(h + 1) * GLA_V_DIM]) + o_inter[:, h * GLA_V_DIM:(h + 1) * GLA_V_DIM]
            outs.append(_rms(oh, ng_ref[:, h * GLA_V_DIM:(h + 1) * GLA_V_DIM]))
        state_ref[...] = (state * jnp.exp(total) + _dot_tn(v, k_st)) * bd_ref[...]
        y_ref[r0:r0 + c, :] = (jnp.concatenate(outs, axis=1) * _silu(g)).astype(BF16)


def _gla(o_gla, o_small, w_gk2, b_gk, norm_g, b, l):
    ns = l // GLA_STEP
    kw = GLA_HEADS * GLA_K_DIM
    vw = GLA_HEADS * GLA_V_DIM
    w2 = jnp.zeros((LANES, kw), F32).at[SMALL_GGK_LANE:SMALL_GGK_LANE + GLA_GATE_RANK].set(w_gk2)
    bd = jnp.asarray(np.arange(vw)[:, None] // GLA_V_DIM == np.arange(kw)[None, :] // GLA_K_DIM, F32)
    row = lambda bi, ci: (bi * ns + ci, 0)
    return pl.pallas_call(
        _gla_kernel,
        out_shape=jax.ShapeDtypeStruct((b * l, vw), BF16),
        grid=(b, ns),
        in_specs=[pl.BlockSpec((GLA_STEP, 2 * kw + 2 * vw), row), pl.BlockSpec((GLA_STEP, LANES), row),
                  _const_spec((LANES, kw)), _const_spec((1, kw)), _const_spec((vw, kw)), _const_spec((1, vw))],
        out_specs=pl.BlockSpec((GLA_STEP, vw), row),
        scratch_shapes=[pltpu.VMEM((vw, kw), F32)],
        compiler_params=_params("arbitrary", "arbitrary"),
        name="gla_mixer",
    )(o_gla, o_small, w2, b_gk.reshape(1, -1), bd, norm_g.reshape(1, -1))


def _mla_proj_kernel(blk_ref, c_ref, s_ref, qn_ref, kvn_ref, wq_ref, wqr_ref, wk_ref, wvt_ref,
                     q_ref, k_ref, vt_ref):
    o_ckv = MLA_Q_LORA
    o_kr = o_ckv + MLA_KV_LORA
    cq = _rms(blk_ref[:, 0:MLA_Q_LORA].astype(F32), qn_ref[...]).astype(BF16)
    ckv = _rms(blk_ref[:, o_ckv:o_kr].astype(F32), kvn_ref[...]).astype(BF16)
    c = c_ref[...]
    s = s_ref[...]
    ct = jnp.concatenate([c] * MLA_HEADS, axis=1)
    st = jnp.concatenate([s] * MLA_HEADS, axis=1)
    scale = (MLA_NOPE + MLA_ROPE) ** -0.5
    q = (_dot(cq, wq_ref[...]) * ct + _dot(cq, wqr_ref[...]) * st) * scale
    q_ref[...] = q.astype(BF16)
    kpe = blk_ref[:, o_kr:o_kr + LANES].astype(F32) * c + blk_ref[:, o_kr + LANES:o_kr + 2 * LANES].astype(F32) * s
    k = _dot(ckv, wk_ref[...]) + jnp.concatenate([kpe] * MLA_HEADS, axis=1)
    k_ref[...] = k.astype(BF16)
    vt_ref[...] = _dot_nt(wvt_ref[...], ckv).astype(BF16)


def _mla_proj(o_mla, c_tab, s_tab, q_norm, kv_norm, wq, wqr, wk, wvt, b, l):
    tm = min(ATTN_BLOCK_K, l)
    nk = l // tm
    hw = MLA_HEADS * MLA_HEAD_PAD
    vw = MLA_HEADS * MLA_V
    row = lambda bi, j: (bi * nk + j, 0)
    return pl.pallas_call(
        _mla_proj_kernel,
        out_shape=(jax.ShapeDtypeStruct((b * l, hw), BF16), jax.ShapeDtypeStruct((b * l, hw), BF16),
                   jax.ShapeDtypeStruct((b, nk, vw, tm), BF16)),
        grid=(b, nk),
        in_specs=[pl.BlockSpec((tm, o_mla.shape[1]), row), pl.BlockSpec((tm, LANES), row),
                  pl.BlockSpec((tm, LANES), row),
                  _const_spec((1, MLA_Q_LORA)), _const_spec((1, MLA_KV_LORA)),
                  _const_spec(wq.shape), _const_spec(wqr.shape), _const_spec(wk.shape), _const_spec(wvt.shape)],
        out_specs=(pl.BlockSpec((tm, hw), row), pl.BlockSpec((tm, hw), row),
                   pl.BlockSpec((None, None, vw, tm), lambda bi, j: (bi, j, 0, 0))),
        compiler_params=_params("arbitrary", "arbitrary"),
        name="mla_proj",
    )(o_mla, c_tab, s_tab, q_norm.reshape(1, -1), kv_norm.reshape(1, -1), wq, wqr, wk, wvt)


def _flash_kernel(q_ref, k_ref, vt_ref, o_ref, m_ref, l_ref, acc_ref):
    bq, bk = ATTN_BLOCK_Q, ATTN_BLOCK_K
    qi = pl.program_id(2)
    q = q_ref[...]
    m_ref[...] = jnp.full(m_ref.shape, -jnp.inf, F32)
    l_ref[...] = jnp.zeros(l_ref.shape, F32)
    acc_ref[...] = jnp.zeros(acc_ref.shape, F32)

    def step(kb, masked):
        k = k_ref[pl.ds(pl.multiple_of(kb * bk, bk), bk), :]
        s = _dot_nt(k, q)
        if masked:
            kpos = lax.broadcasted_iota(jnp.int32, (bk, bq), 0)
            qpos = lax.broadcasted_iota(jnp.int32, (bk, bq), 1)
            s = jnp.where(kpos <= qpos, s, -jnp.inf)
        m_old = m_ref[...]
        m_new = jnp.maximum(m_old, jnp.max(s, axis=0, keepdims=True))
        alpha = jnp.exp(m_old - m_new)
        p = jnp.exp(s - m_new)
        l_ref[...] = alpha * l_ref[...] + jnp.sum(p, axis=0, keepdims=True)
        acc_ref[...] = acc_ref[...] * alpha + _dot(vt_ref[kb], p.astype(BF16))
        m_ref[...] = m_new

    def body(kb, carry):
        step(kb, False)
        return carry

    lax.fori_loop(0, qi, body, 0)
    step(qi, True)
    o_ref[...] = (acc_ref[...] / l_ref[...]).astype(BF16)


def _flash(q, k, vt, b, l):
    bq, bk = ATTN_BLOCK_Q, ATTN_BLOCK_K
    nq = l // bq
    nk = l // bk
    q3 = q.reshape(b, l, MLA_HEADS * MLA_HEAD_PAD)
    k3 = k.reshape(b, l, MLA_HEADS * MLA_HEAD_PAD)
    return pl.pallas_call(
        _flash_kernel,
        out_shape=jax.ShapeDtypeStruct((b, MLA_HEADS * MLA_V, l), BF16),
        grid=(b, MLA_HEADS, nq),
        in_specs=[pl.BlockSpec((None, bq, MLA_HEAD_PAD), lambda bi, h, qi: (bi, qi, h)),
                  pl.BlockSpec((None, l, MLA_HEAD_PAD), lambda bi, h, qi: (bi, 0, h)),
                  pl.BlockSpec((None, nk, MLA_V, bk), lambda bi, h, qi: (bi, 0, h, 0))],
        out_specs=pl.BlockSpec((None, MLA_V, bq), lambda bi, h, qi: (bi, h, qi)),
        scratch_shapes=[pltpu.VMEM((1, bq), F32), pltpu.VMEM((1, bq), F32), pltpu.VMEM((MLA_V, bq), F32)],
        compiler_params=_params("arbitrary", "arbitrary", "arbitrary"),
        name="mla_flash",
    )(q3, k3, vt)


def _merge_kernel(x_ref, yssd_ref, yret_ref, ymt_ref, ygla_ref, gpre_ref, wg_ref, bg_ref, wb_ref, wo_ref,
                  gpost_ref, o_ref):
    x = x_ref[...]
    h = _rms(x, gpre_ref[...]).astype(BF16)
    branches = (_dot(yssd_ref[...], wb_ref[0]), _dot(yret_ref[...], wb_ref[1]),
                _dot_tn(ymt_ref[...], wb_ref[2]), _dot(ygla_ref[...], wb_ref[3]))
    merged = None
    for i, br in enumerate(branches):
        gate = jax.nn.sigmoid(_dot(h, wg_ref[:, i * D_MODEL:(i + 1) * D_MODEL]) + bg_ref[:, i * D_MODEL:(i + 1) * D_MODEL])
        merged = gate * br if merged is None else merged + gate * br
    o = _dot(merged.astype(BF16), wo_ref[...])
    o_ref[...] = x + _rms(o, gpost_ref[...])


def _merge(x3, y_ssd, y_ret, y_mla_t, y_gla, g_pre, wg, b_gate, wb, wo, g_post):
    b, l, _ = x3.shape
    tm = min(TOKEN_TILE, l)
    nt = l // tm
    w = BRANCH_WIDTH
    tok = lambda width: pl.BlockSpec((None, tm, width), lambda bi, j: (bi, j, 0))
    return pl.pallas_call(
        _merge_kernel,
        out_shape=jax.ShapeDtypeStruct((b, l, D_MODEL), F32),
        grid=(b, nt),
        in_specs=[tok(D_MODEL), tok(w), tok(w),
                  pl.BlockSpec((None, w, tm), lambda bi, j: (bi, 0, j)), tok(w),
                  _const_spec((1, D_MODEL)), _const_spec(wg.shape), _const_spec((1, N_BRANCHES * D_MODEL)),
                  _const_spec(wb.shape), _const_spec(wo.shape), _const_spec((1, D_MODEL))],
        out_specs=tok(D_MODEL),
        compiler_params=_params("arbitrary", "arbitrary"),
        name="merge_out",
    )(x3, y_ssd.reshape(b, l, w), y_ret.reshape(b, l, w), y_mla_t, y_gla.reshape(b, l, w),
      g_pre, wg, b_gate.reshape(1, -1), wb, wo, g_post)


def _mlp_kernel(x_ref, gpre_ref, wi_ref, wo_ref, gpost_ref, o_ref):
    x = x_ref[...]
    h = _rms(x, gpre_ref[...]).astype(BF16)
    acc = None
    for j in range(D_FF // D_MODEL):
        u = jnp.maximum(_dot(h, wi_ref[:, j * D_MODEL:(j + 1) * D_MODEL]), 0.0)
        part = _dot((u * u).astype(BF16), wo_ref[j * D_MODEL:(j + 1) * D_MODEL, :])
        acc = part if acc is None else acc + part
    o_ref[...] = x + _rms(acc, gpost_ref[...])


def _mlp(x2, g_pre, wi, wo, g_post):
    t = x2.shape[0]
    tm = min(TOKEN_TILE, t)
    return pl.pallas_call(
        _mlp_kernel,
        out_shape=jax.ShapeDtypeStruct((t, D_MODEL), F32),
        grid=(t // tm,),
        in_specs=[pl.BlockSpec((tm, D_MODEL), lambda i: (i, 0)), _const_spec((1, D_MODEL)),
                  _const_spec(wi.shape), _const_spec(wo.shape), _const_spec((1, D_MODEL))],
        out_specs=pl.BlockSpec((tm, D_MODEL), lambda i: (i, 0)),
        compiler_params=_params("arbitrary"),
        name="mlp",
    )(x2, g_pre, wi, wo, g_post)


def _rot_perm(n_heads, dim):
    half = dim // 2
    return np.concatenate([h * dim + (np.arange(dim) + half) % dim for h in range(n_heads)])


def _layer_weights(w_in, mla_w_uq, mla_w_ukv):
    col = lambda a, b: w_in[:, a:b]
    w_ssd = col(_O_Z, _O_DT).astype(BF16)
    rq, rk = col(_O_RQ, _O_RK), col(_O_RK, _O_RV)
    perm = _rot_perm(RET_HEADS, RET_QK_DIM)
    w_ret = jnp.concatenate([rq, rk, rq[:, perm], rk[:, perm], col(_O_RV, _O_CQ)], axis=1).astype(BF16)
    kr = col(_O_KR, _O_GQ)
    zl = jnp.zeros((D_MODEL, MLA_PE_LANE), F32)
    zr = jnp.zeros((D_MODEL, LANES - MLA_PE_LANE - MLA_ROPE), F32)
    kperm = _rot_perm(1, MLA_ROPE)
    w_mla = jnp.concatenate([col(_O_CQ, _O_KR), zl, kr, zr, zl, kr[:, kperm], zr], axis=1).astype(BF16)
    w_gla = jnp.concatenate([col(_O_GQ, _O_GGK), col(_O_GG, _O_GATE)], axis=1).astype(BF16)
    w_small = jnp.concatenate(
        [col(_O_DT, _O_RQ), col(_O_GGK, _O_GG),
         jnp.zeros((D_MODEL, LANES - SSD_HEADS - GLA_GATE_RANK), F32)], axis=1).astype(BF16)
    w_gate = col(_O_GATE, _O_END).astype(BF16)

    hd = MLA_NOPE + MLA_ROPE
    uq = mla_w_uq.reshape(MLA_Q_LORA, MLA_HEADS, hd)
    padq = jnp.zeros((MLA_Q_LORA, MLA_HEADS, MLA_HEAD_PAD - hd), F32)
    wq = jnp.concatenate([uq, padq], axis=2).reshape(MLA_Q_LORA, -1).astype(BF16)
    uq_rot = uq[:, :, MLA_NOPE + kperm]
    wqr = jnp.concatenate([jnp.zeros((MLA_Q_LORA, MLA_HEADS, MLA_NOPE), F32), uq_rot, padq],
                          axis=2).reshape(MLA_Q_LORA, -1).astype(BF16)
    ukv = mla_w_ukv.reshape(MLA_KV_LORA, MLA_HEADS, MLA_NOPE + MLA_V)
    wk = jnp.concatenate([ukv[:, :, :MLA_NOPE],
                          jnp.zeros((MLA_KV_LORA, MLA_HEADS, MLA_HEAD_PAD - MLA_NOPE), F32)],
                         axis=2).reshape(MLA_KV_LORA, -1).astype(BF16)
    wvt = ukv[:, :, MLA_NOPE:].reshape(MLA_KV_LORA, -1).T.astype(BF16)
    return w_ssd, w_ret, w_mla, w_gla, w_small, w_gate, wq, wqr, wk, wvt


def _layer(x3, tabs, w_in, b_gate, ssd_conv_w, ssd_conv_b, ssd_dt_bias, ssd_a_log, ssd_d, ssd_norm, ret_norm,
           mla_q_norm, mla_w_uq, mla_kv_norm, mla_w_ukv, gla_w_gk2, gla_b_gk, gla_norm, w_branch, w_out,
           norm_pre_mix, norm_post_mix, norm_pre_mlp, norm_post_mlp, w_mlp_in, w_mlp_out):
    b, l, d = x3.shape
    c_ret, s_ret, c_mla, s_mla = tabs
    w_ssd, w_ret, w_mla, w_gla, w_small, w_gate, wq, wqr, wk, wvt = _layer_weights(w_in, mla_w_uq, mla_w_ukv)
    g_pre = norm_pre_mix.reshape(1, -1)
    x2 = x3.reshape(b * l, d)
    o_ssd, o_ret, o_mla, o_gla, o_small = _inproj(x2, g_pre, w_ssd, w_ret, w_mla, w_gla, w_small)
    y_ssd = _ssd(o_ssd, o_small, ssd_conv_w, ssd_conv_b, ssd_dt_bias, ssd_a_log, ssd_d, ssd_norm, b, l)
    y_ret = _ret(o_ret, c_ret, s_ret, ret_norm, b, l)
    q, k, vt = _mla_proj(o_mla, c_mla, s_mla, mla_q_norm, mla_kv_norm, wq, wqr, wk, wvt, b, l)
    y_mla_t = _flash(q, k, vt, b, l)
    y_gla = _gla(o_gla, o_small, gla_w_gk2, gla_b_gk, gla_norm, b, l)
    x3 = _merge(x3, y_ssd, y_ret, y_mla_t, y_gla, g_pre, w_gate, b_gate, w_branch.astype(BF16),
                w_out.astype(BF16), norm_post_mix.reshape(1, -1))
    x2 = _mlp(x3.reshape(b * l, d), norm_pre_mlp.reshape(1, -1), w_mlp_in.astype(BF16),
              w_mlp_out.astype(BF16), norm_post_mlp.reshape(1, -1))
    return x2.reshape(b, l, d)


def kernel(x, positions, w_in, b_gate, ssd_conv_w, ssd_conv_b, ssd_dt_bias, ssd_a_log, ssd_d, ssd_norm, ret_norm, mla_q_norm, mla_w_uq, mla_kv_norm, mla_w_ukv, gla_w_gk2, gla_b_gk, gla_norm, w_branch, w_out, norm_pre_mix, norm_post_mix, norm_pre_mlp, norm_post_mlp, w_mlp_in, w_mlp_out):
    per_layer = (w_in, b_gate, ssd_conv_w, ssd_conv_b, ssd_dt_bias, ssd_a_log, ssd_d, ssd_norm, ret_norm,
                 mla_q_norm, mla_w_uq, mla_kv_norm, mla_w_ukv, gla_w_gk2, gla_b_gk, gla_norm, w_branch, w_out,
                 norm_pre_mix, norm_post_mix, norm_pre_mlp, norm_post_mlp, w_mlp_in, w_mlp_out)
    tabs = _rope_tables(positions)
    for i in range(w_in.shape[0]):
        x = _layer(x, tabs, *(p[i] for p in per_layer))
    return x
```

```python
import functools

import numpy as np
import jax
import jax.numpy as jnp
from jax import lax
from jax.experimental import pallas as pl
from jax.experimental.pallas import tpu as pltpu

F32 = jnp.float32
BF16 = jnp.bfloat16

D_MODEL = 1024
SSD_HEADS = 8
SSD_HEAD_DIM = 64
SSD_INNER = 512
SSD_GROUPS = 2
SSD_STATE = 128
SSD_CONV = 4
SSD_CHUNK = 128
SSD_XBC = 1024
RET_HEADS = 4
RET_QK_DIM = 64
RET_V_DIM = 128
RET_CHUNK = 128
MLA_HEADS = 8
MLA_Q_LORA = 256
MLA_KV_LORA = 128
MLA_NOPE = 64
MLA_ROPE = 32
MLA_V = 64
GLA_HEADS = 4
GLA_K_DIM = 64
GLA_V_DIM = 128
GLA_GATE_RANK = 16
GLA_GATE_NORM = 16.0
GLA_CHUNK = 64
N_BRANCHES = 4
BRANCH_WIDTH = 512
D_FF = 4 * D_MODEL
ROPE_THETA = 10000.0
RMS_EPS = 1e-6

LANES = 128
MLA_HEAD_PAD = 128
VT_ROWS = MLA_V + 16
LOG2_E = 1.4426950408889634
VMEM_LIMIT = 56 * 1024 * 1024

TOKEN_TILE = 512
ATTN_BLOCK_Q = 512
ATTN_BLOCK_K = 512
GLA_STEP = 128

_SIZES = (SSD_INNER, SSD_XBC, SSD_HEADS,
          RET_HEADS * RET_QK_DIM, RET_HEADS * RET_QK_DIM, RET_HEADS * RET_V_DIM, RET_HEADS * RET_V_DIM,
          MLA_Q_LORA, MLA_KV_LORA, MLA_ROPE,
          GLA_HEADS * GLA_K_DIM, GLA_HEADS * GLA_K_DIM, GLA_HEADS * GLA_V_DIM, GLA_GATE_RANK,
          GLA_HEADS * GLA_V_DIM, N_BRANCHES * D_MODEL)
_OFFS = tuple(int(v) for v in np.concatenate([[0], np.cumsum(_SIZES)]))
(_O_Z, _O_XBC, _O_DT, _O_RQ, _O_RK, _O_RV, _O_RG, _O_CQ, _O_CKV, _O_KR,
 _O_GQ, _O_GK, _O_GV, _O_GGK, _O_GG, _O_GATE, _O_END) = _OFFS

SMALL_DT_LANE = 0
SMALL_GGK_LANE = 8
MLA_PE_LANE = MLA_NOPE


def _dot(a, b, precision=None):
    return jnp.dot(a, b, preferred_element_type=F32, precision=precision)


def _dot_nt(a, b):
    return lax.dot_general(a, b, (((1,), (1,)), ((), ())), preferred_element_type=F32)


def _dot_tn(a, b):
    return lax.dot_general(a, b, (((0,), (0,)), ((), ())), preferred_element_type=F32)


def _rms(x, g):
    return x * lax.rsqrt(jnp.mean(x * x, axis=-1, keepdims=True) + RMS_EPS) * g


def _silu(x):
    return x * jax.nn.sigmoid(x)


def _softplus(x):
    return jnp.maximum(x, 0.0) + jnp.log1p(jnp.exp(-jnp.abs(x)))


def _params(*sem):
    return pltpu.CompilerParams(dimension_semantics=sem, vmem_limit_bytes=VMEM_LIMIT)


def _const_spec(shape):
    nd = len(shape)
    return pl.BlockSpec(shape, lambda *_: (0,) * nd, pipeline_mode=pl.Buffered(1))


def _tril(n):
    r = lax.broadcasted_iota(jnp.int32, (n, n), 0)
    c = lax.broadcasted_iota(jnp.int32, (n, n), 1)
    return r >= c


def _rope_kernel(pos_ref, c_ref, cr_ref, sr_ref, cm_ref, sm_ref):
    pos = pos_ref[...]
    c = c_ref[...]
    ang_r = pos * c[0:1, :]
    cr_ref[...] = jnp.cos(ang_r)
    sr_ref[...] = jnp.sin(ang_r) * c[1:2, :]
    ang_m = pos * c[2:3, :]
    cm_ref[...] = jnp.cos(ang_m)
    sm_ref[...] = jnp.sin(ang_m) * c[3:4, :]


def _rope_consts():
    lane = np.arange(LANES)
    half_r = RET_QK_DIM // 2
    half_m = MLA_ROPE // 2
    inv_r = ROPE_THETA ** (-jnp.arange(half_r, dtype=F32) / half_r)
    inv_m = ROPE_THETA ** (-jnp.arange(half_m, dtype=F32) / half_m)
    freq_r = inv_r[lane % half_r]
    sign_r = jnp.asarray(np.where((lane % RET_QK_DIM) < half_r, -1.0, 1.0), F32)
    in_pe = (lane >= MLA_PE_LANE) & (lane < MLA_PE_LANE + MLA_ROPE)
    freq_m = jnp.where(in_pe, inv_m[(lane - MLA_PE_LANE) % half_m], 0.0)
    sign_m = jnp.asarray(np.where(in_pe, np.where((lane - MLA_PE_LANE) < half_m, -1.0, 1.0), 0.0), F32)
    z = jnp.zeros((LANES,), F32)
    return jnp.stack([freq_r, sign_r, freq_m, sign_m, z, z, z, z]).astype(F32)


def _rope_tables(positions):
    b, l = positions.shape
    t = b * l
    tb = min(1024, t)
    pos = positions.astype(F32).reshape(t, 1)
    tab = jax.ShapeDtypeStruct((t, LANES), F32)
    spec = pl.BlockSpec((tb, LANES), lambda i: (i, 0))
    return pl.pallas_call(
        _rope_kernel,
        out_shape=(tab, tab, tab, tab),
        grid=(t // tb,),
        in_specs=[pl.BlockSpec((tb, 1), lambda i: (i, 0)), _const_spec((8, LANES))],
        out_specs=(spec, spec, spec, spec),
        compiler_params=_params("arbitrary"),
        name="rope_tables",
    )(pos, _rope_consts())


def _inproj_kernel(x_ref, g_ref, wssd_ref, wret_ref, wmla_ref, wgla_ref, wsm_ref,
                   ossd_ref, oret_ref, omla_ref, ogla_ref, osm_ref):
    h = _rms(x_ref[...], g_ref[...]).astype(BF16)
    ossd_ref[...] = _dot(h, wssd_ref[...]).astype(BF16)
    oret_ref[...] = _dot(h, wret_ref[...]).astype(BF16)
    omla_ref[...] = _dot(h, wmla_ref[...]).astype(BF16)
    ogla_ref[...] = _dot(h, wgla_ref[...]).astype(BF16)
    osm_ref[...] = _dot(h, wsm_ref[...])


def _inproj(x2, g, wssd, wret, wmla, wgla, wsm):
    t = x2.shape[0]
    tm = min(TOKEN_TILE, t)
    ws = (wssd, wret, wmla, wgla, wsm)
    outs = tuple(jax.ShapeDtypeStruct((t, w.shape[1]), BF16) for w in ws[:4]) + (
        jax.ShapeDtypeStruct((t, LANES), F32),)
    return pl.pallas_call(
        _inproj_kernel,
        out_shape=outs,
        grid=(t // tm,),
        in_specs=[pl.BlockSpec((tm, D_MODEL), lambda i: (i, 0)), _const_spec((1, D_MODEL))]
                 + [_const_spec(w.shape) for w in ws],
        out_specs=tuple(pl.BlockSpec((tm, o.shape[1]), lambda i: (i, 0)) for o in outs),
        compiler_params=_params("arbitrary"),
        name="in_proj",
    )(x2, g, *ws)


def _ssd_kernel(blk_ref, sm_ref, cw_ref, cb_ref, dtb_ref, alog_ref, exp_ref, dsk_ref, ng_ref,
                y_ref, conv_ref, state_ref):
    q = SSD_CHUNK
    tail = 8

    @pl.when(pl.program_id(1) == 0)
    def _():
        conv_ref[0:tail, :] = jnp.zeros((tail, SSD_XBC), F32)
        state_ref[...] = jnp.zeros_like(state_ref)

    z = blk_ref[:, 0:SSD_INNER].astype(F32)
    xbc = blk_ref[:, SSD_INNER:SSD_INNER + SSD_XBC].astype(F32)
    conv_ref[tail:tail + q, :] = xbc
    acc = cb_ref[...] + cw_ref[SSD_CONV - 1:SSD_CONV, :] * xbc
    for j in range(SSD_CONV - 1):
        off = tail - (SSD_CONV - 1) + j
        acc = acc + cw_ref[j:j + 1, :] * conv_ref[off:off + q, :]
    conv_ref[0:tail, :] = xbc[q - tail:q, :]
    xbc = _silu(acc)
    xs = xbc[:, 0:SSD_INNER]
    gn = SSD_GROUPS * SSD_STATE
    bm = xbc[:, SSD_INNER:SSD_INNER + gn].astype(BF16)
    cm = xbc[:, SSD_INNER + gn:SSD_INNER + 2 * gn].astype(BF16)

    hi = lax.Precision.HIGHEST
    dt = _softplus(sm_ref[...] + dtb_ref[...])
    la = dt * (-jnp.exp(alog_ref[...]))
    mask = _tril(q)
    cum = _dot(mask.astype(F32), la, hi)
    cum_t = cum.T
    expand = exp_ref[...]
    cum_ch = _dot(cum, expand, hi)
    dt_ch = _dot(dt, expand, hi)
    xdt = xs * dt_ch
    xdt_b = xdt.astype(BF16)
    last = cum_ch[q - 1:q, :]
    xdte_b = (xdt * jnp.exp(last - cum_ch)).astype(BF16)
    chunk_decay = jnp.exp(last)

    hpg = SSD_HEADS // SSD_GROUPS
    gw = hpg * SSD_HEAD_DIM
    lane_lo = lax.broadcasted_iota(jnp.int32, (q, LANES), 1) < SSD_HEAD_DIM
    ydiag, yoff = [], []
    for g in range(SSD_GROUPS):
        cg = cm[:, g * SSD_STATE:(g + 1) * SSD_STATE]
        bg = bm[:, g * SSD_STATE:(g + 1) * SSD_STATE]
        s = _dot_nt(cg, bg)
        st = state_ref[g]
        yoff.append(_dot(cg, st.astype(BF16)))
        for j in range(hpg // 2):
            xpair = xdt_b[:, g * gw + j * LANES:g * gw + (j + 1) * LANES]
            ys = []
            for e in range(2):
                h = g * hpg + 2 * j + e
                seg = cum[:, h:h + 1] - cum_t[h:h + 1, :]
                a = (s * jnp.exp(jnp.where(mask, seg, -jnp.inf))).astype(BF16)
                ys.append(_dot(a, xpair))
            ydiag.append(jnp.where(lane_lo, ys[0], ys[1]))
        state_ref[g] = st * chunk_decay[:, g * gw:(g + 1) * gw] + _dot_tn(bg, xdte_b[:, g * gw:(g + 1) * gw])
    y = jnp.concatenate(ydiag, axis=1) + jnp.concatenate(yoff, axis=1) * jnp.exp(cum_ch)
    y = (y + dsk_ref[...] * xs) * _silu(z)
    outs = []
    for g in range(SSD_GROUPS):
        outs.append(_rms(y[:, g * gw:(g + 1) * gw], ng_ref[:, g * gw:(g + 1) * gw]))
    y_ref[...] = jnp.concatenate(outs, axis=1).astype(BF16)


def _ssd(o_ssd, o_small, conv_w, conv_b, dt_bias, a_log, d_skip, norm_g, b, l):
    q = SSD_CHUNK
    nc = l // q
    pad = LANES - SSD_HEADS
    dtb = jnp.pad(dt_bias, (0, pad)).reshape(1, LANES)
    alog = jnp.pad(a_log, (0, pad)).reshape(1, LANES)
    head_of = np.arange(SSD_INNER) // SSD_HEAD_DIM
    expand = jnp.asarray(np.arange(LANES)[:, None] == head_of[None, :], F32)
    dsk = jnp.repeat(d_skip, SSD_HEAD_DIM).reshape(1, SSD_INNER)
    row = lambda bi, ci: (bi * nc + ci, 0)
    return pl.pallas_call(
        _ssd_kernel,
        out_shape=jax.ShapeDtypeStruct((b * l, SSD_INNER), BF16),
        grid=(b, nc),
        in_specs=[pl.BlockSpec((q, SSD_INNER + SSD_XBC), row), pl.BlockSpec((q, LANES), row),
                  _const_spec((SSD_CONV, SSD_XBC)), _const_spec((1, SSD_XBC)),
                  _const_spec((1, LANES)), _const_spec((1, LANES)), _const_spec((LANES, SSD_INNER)),
                  _const_spec((1, SSD_INNER)), _const_spec((1, SSD_INNER))],
        out_specs=pl.BlockSpec((q, SSD_INNER), row),
        scratch_shapes=[pltpu.VMEM((8 + q, SSD_XBC), F32),
                        pltpu.VMEM((SSD_GROUPS, SSD_STATE, SSD_INNER // SSD_GROUPS), F32)],
        compiler_params=_params("arbitrary", "arbitrary"),
        name="ssd_mixer",
    )(o_ssd, o_small, conv_w, conv_b.reshape(1, -1), dtb, alog, expand, dsk, norm_g.reshape(1, -1))


def _ret_kernel(blk_ref, c_ref, s_ref, dm_ref, qs_ref, ks_ref, cd_ref, bd_ref, ng_ref, y_ref, state_ref):
    @pl.when(pl.program_id(1) == 0)
    def _():
        state_ref[...] = jnp.zeros_like(state_ref)

    w = RET_HEADS * RET_QK_DIM
    vw = RET_HEADS * RET_V_DIM
    ct = jnp.concatenate([c_ref[...]] * (w // LANES), axis=1)
    st = jnp.concatenate([s_ref[...]] * (w // LANES), axis=1)
    q = blk_ref[:, 0:w].astype(F32) * ct + blk_ref[:, 2 * w:3 * w].astype(F32) * st
    k = (blk_ref[:, w:2 * w].astype(F32) * ct + blk_ref[:, 3 * w:4 * w].astype(F32) * st) * (RET_QK_DIM ** -0.5)
    v = blk_ref[:, 4 * w:4 * w + vw]
    g = blk_ref[:, 4 * w + vw:4 * w + 2 * vw].astype(F32)
    k_b = k.astype(BF16)
    lane = lax.broadcasted_iota(jnp.int32, (1, w), 1)
    state = state_ref[...]
    y_off = _dot((q * qs_ref[...]).astype(BF16), state.astype(BF16))
    ys = []
    for h in range(RET_HEADS):
        in_head = (lane >= h * RET_QK_DIM) & (lane < (h + 1) * RET_QK_DIM)
        qh = jnp.where(in_head, q, 0.0).astype(BF16)
        sc = (_dot_nt(qh, k_b) * dm_ref[h]).astype(BF16)
        vh = v[:, h * RET_V_DIM:(h + 1) * RET_V_DIM]
        yh = _dot(sc, vh) + y_off[:, h * RET_V_DIM:(h + 1) * RET_V_DIM]
        ys.append(_rms(yh, ng_ref[:, h * RET_V_DIM:(h + 1) * RET_V_DIM]))
    new_state = state * cd_ref[...] + _dot_tn((k * ks_ref[...]).astype(BF16), v)
    state_ref[...] = new_state * bd_ref[...]
    y_ref[...] = (jnp.concatenate(ys, axis=1) * _silu(g)).astype(BF16)


def _ret_consts():
    q = RET_CHUNK
    lg = np.log1p(-np.exp2(-5.0 - np.arange(RET_HEADS, dtype=np.float64)))
    i = np.arange(q)
    dm = np.where(i[:, None] >= i[None, :], np.exp(lg[:, None, None] * (i[:, None] - i[None, :])[None]), 0.0)
    qs = np.repeat(np.exp(lg[None, :] * (i[:, None] + 1)), RET_QK_DIM, axis=1)
    ks = np.repeat(np.exp(lg[None, :] * (q - 1 - i[:, None])), RET_QK_DIM, axis=1)
    cd = np.repeat(np.exp(lg * q), RET_V_DIM)[None, :]
    bd = (np.arange(RET_HEADS * RET_QK_DIM)[:, None] // RET_QK_DIM
          == np.arange(RET_HEADS * RET_V_DIM)[None, :] // RET_V_DIM)
    return tuple(jnp.asarray(a, F32) for a in (dm, qs, ks, cd, bd))


def _ret(o_ret, c_tab, s_tab, norm_g, b, l):
    q = RET_CHUNK
    nc = l // q
    w = RET_HEADS * RET_QK_DIM
    vw = RET_HEADS * RET_V_DIM
    dm, qs, ks, cd, bd = _ret_consts()
    row = lambda bi, ci: (bi * nc + ci, 0)
    return pl.pallas_call(
        _ret_kernel,
        out_shape=jax.ShapeDtypeStruct((b * l, vw), BF16),
        grid=(b, nc),
        in_specs=[pl.BlockSpec((q, 4 * w + 2 * vw), row), pl.BlockSpec((q, LANES), row),
                  pl.BlockSpec((q, LANES), row),
                  _const_spec(dm.shape), _const_spec(qs.shape), _const_spec(ks.shape),
                  _const_spec(cd.shape), _const_spec(bd.shape), _const_spec((1, vw))],
        out_specs=pl.BlockSpec((q, vw), row),
        scratch_shapes=[pltpu.VMEM((w, vw), F32)],
        compiler_params=_params("arbitrary", "arbitrary"),
        name="ret_mixer",
    )(o_ret, c_tab, s_tab, dm, qs, ks, cd, bd, norm_g.reshape(1, -1))


def _gla_kernel(blk_ref, sm_ref, w2_ref, bgk_ref, bd_ref, ng_ref, y_ref, state_ref):
    @pl.when(pl.program_id(1) == 0)
    def _():
        state_ref[...] = jnp.zeros_like(state_ref)

    c = GLA_CHUNK
    kw = GLA_HEADS * GLA_K_DIM
    vw = GLA_HEADS * GLA_V_DIM
    hi = lax.Precision.HIGHEST
    logits = _dot(sm_ref[...], w2_ref[...], hi) + bgk_ref[...]
    log_g = -_softplus(-logits) * (1.0 / GLA_GATE_NORM)
    mask = _tril(c)
    tri = mask.astype(F32)
    lane = lax.broadcasted_iota(jnp.int32, (1, kw), 1)
    for s in range(GLA_STEP // c):
        r0 = s * c
        qv = blk_ref[r0:r0 + c, 0:kw].astype(F32) * (GLA_K_DIM ** -0.5)
        kv = blk_ref[r0:r0 + c, kw:2 * kw].astype(F32)
        v = blk_ref[r0:r0 + c, 2 * kw:2 * kw + vw]
        g = blk_ref[r0:r0 + c, 2 * kw + vw:2 * kw + 2 * vw].astype(F32)
        cum = _dot(tri, log_g[r0:r0 + c, :], hi)
        total = cum[c - 1:c, :]
        q_in = qv * jnp.exp(cum)
        k_in = (kv * jnp.exp(-cum)).astype(BF16)
        k_st = (kv * jnp.exp(total - cum)).astype(BF16)
        state = state_ref[...]
        o_inter = _dot_nt(q_in.astype(BF16), state.astype(BF16))
        outs = []
        for h in range(GLA_HEADS):
            in_head = (lane >= h * GLA_K_DIM) & (lane < (h + 1) * GLA_K_DIM)
            qh = jnp.where(in_head, q_in, 0.0).astype(BF16)
            sc = jnp.where(mask, _dot_nt(qh, k_in), 0.0).astype(BF16)
            oh = _dot(sc, v[:, h * GLA_V_DIM:(h + 1) * GLA_V_DIM]) + o_inter[:, h * GLA_V_DIM:(h + 1) * GLA_V_DIM]
            outs.append(_rms(oh, ng_ref[:, h * GLA_V_DIM:(h + 1) * GLA_V_DIM]))
        state_ref[...] = (state * jnp.exp(total) + _dot_tn(v, k_st)) * bd_ref[...]
        y_ref[r0:r0 + c, :] = (jnp.concatenate(outs, axis=1) * _silu(g)).astype(BF16)


def _gla(o_gla, o_small, w_gk2, b_gk, norm_g, b, l):
    ns = l // GLA_STEP
    kw = GLA_HEADS * GLA_K_DIM
    vw = GLA_HEADS * GLA_V_DIM
    w2 = jnp.zeros((LANES, kw), F32).at[SMALL_GGK_LANE:SMALL_GGK_LANE + GLA_GATE_RANK].set(w_gk2)
    bd = jnp.asarray(np.arange(vw)[:, None] // GLA_V_DIM == np.arange(kw)[None, :] // GLA_K_DIM, F32)
    row = lambda bi, ci: (bi * ns + ci, 0)
    return pl.pallas_call(
        _gla_kernel,
        out_shape=jax.ShapeDtypeStruct((b * l, vw), BF16),
        grid=(b, ns),
        in_specs=[pl.BlockSpec((GLA_STEP, 2 * kw + 2 * vw), row), pl.BlockSpec((GLA_STEP, LANES), row),
                  _const_spec((LANES, kw)), _const_spec((1, kw)), _const_spec((vw, kw)), _const_spec((1, vw))],
        out_specs=pl.BlockSpec((GLA_STEP, vw), row),
        scratch_shapes=[pltpu.VMEM((vw, kw), F32)],
        compiler_params=_params("arbitrary", "arbitrary"),
        name="gla_mixer",
    )(o_gla, o_small, w2, b_gk.reshape(1, -1), bd, norm_g.reshape(1, -1))


def _mla_proj_kernel(blk_ref, c_ref, s_ref, qn_ref, kvn_ref, wq_ref, wqr_ref, wk_ref, wvt_ref,
                     q_ref, k_ref, vt_ref):
    o_ckv = MLA_Q_LORA
    o_kr = o_ckv + MLA_KV_LORA
    cq = _rms(blk_ref[:, 0:MLA_Q_LORA].astype(F32), qn_ref[...]).astype(BF16)
    ckv = _rms(blk_ref[:, o_ckv:o_kr].astype(F32), kvn_ref[...]).astype(BF16)
    c = c_ref[...]
    s = s_ref[...]
    ct = jnp.concatenate([c] * MLA_HEADS, axis=1)
    st = jnp.concatenate([s] * MLA_HEADS, axis=1)
    scale = (MLA_NOPE + MLA_ROPE) ** -0.5 * LOG2_E
    q = (_dot(cq, wq_ref[...]) * ct + _dot(cq, wqr_ref[...]) * st) * scale
    q_ref[...] = q.astype(BF16)
    kpe = blk_ref[:, o_kr:o_kr + LANES].astype(F32) * c + blk_ref[:, o_kr + LANES:o_kr + 2 * LANES].astype(F32) * s
    k = _dot(ckv, wk_ref[...]) + jnp.concatenate([kpe] * MLA_HEADS, axis=1)
    k_ref[...] = k.astype(BF16)
    vt = _dot_nt(wvt_ref[...], ckv).astype(BF16)
    ones = jnp.ones((VT_ROWS - MLA_V, vt.shape[1]), BF16)
    for h in range(MLA_HEADS):
        vt_ref[h, 0:MLA_V, :] = vt[h * MLA_V:(h + 1) * MLA_V, :]
        vt_ref[h, MLA_V:VT_ROWS, :] = ones


def _mla_proj(o_mla, c_tab, s_tab, q_norm, kv_norm, wq, wqr, wk, wvt, b, l):
    tm = min(ATTN_BLOCK_K, l)
    nk = l // tm
    hw = MLA_HEADS * MLA_HEAD_PAD
    row = lambda bi, j: (bi * nk + j, 0)
    return pl.pallas_call(
        _mla_proj_kernel,
        out_shape=(jax.ShapeDtypeStruct((b * l, hw), BF16), jax.ShapeDtypeStruct((b * l, hw), BF16),
                   jax.ShapeDtypeStruct((b, nk, MLA_HEADS, VT_ROWS, tm), BF16)),
        grid=(b, nk),
        in_specs=[pl.BlockSpec((tm, o_mla.shape[1]), row), pl.BlockSpec((tm, LANES), row),
                  pl.BlockSpec((tm, LANES), row),
                  _const_spec((1, MLA_Q_LORA)), _const_spec((1, MLA_KV_LORA)),
                  _const_spec(wq.shape), _const_spec(wqr.shape), _const_spec(wk.shape), _const_spec(wvt.shape)],
        out_specs=(pl.BlockSpec((tm, hw), row), pl.BlockSpec((tm, hw), row),
                   pl.BlockSpec((None, None, MLA_HEADS, VT_ROWS, tm), lambda bi, j: (bi, j, 0, 0, 0))),
        compiler_params=_params("arbitrary", "arbitrary"),
        name="mla_proj",
    )(o_mla, c_tab, s_tab, q_norm.reshape(1, -1), kv_norm.reshape(1, -1), wq, wqr, wk, wvt)


def _flash_kernel(q_ref, k_ref, vt_ref, o_ref, m_ref, acc_ref, s0_ref, s1_ref, mb0_ref, mb1_ref):
    bq, bk = ATTN_BLOCK_Q, ATTN_BLOCK_K
    qi = pl.program_id(2)
    q = q_ref[...]
    m_ref[...] = jnp.full(m_ref.shape, -jnp.inf, F32)
    acc_ref[...] = jnp.zeros(acc_ref.shape, F32)
    slots = ((s0_ref, mb0_ref), (s1_ref, mb1_ref))

    def produce(kb, slot, masked):
        s_ref, mb_ref = slots[slot]
        k = k_ref[pl.ds(pl.multiple_of(kb * bk, bk), bk), :]
        s = _dot_nt(k, q)
        if masked:
            kpos = lax.broadcasted_iota(jnp.int32, (bk, bq), 0)
            qpos = lax.broadcasted_iota(jnp.int32, (bk, bq), 1)
            s = jnp.where(kpos <= qpos, s, -jnp.inf)
        s_ref[...] = s
        mb_ref[...] = jnp.max(s, axis=0, keepdims=True)

    def consume(kb, slot):
        s_ref, mb_ref = slots[slot]
        m_old = m_ref[...]
        m_new = jnp.maximum(m_old, mb_ref[...])
        alpha = jnp.exp2(m_old - m_new)
        p = jnp.exp2(s_ref[...] - m_new).astype(BF16)
        acc_ref[...] = acc_ref[...] * alpha + _dot(vt_ref[kb], p)
        m_ref[...] = m_new

    produce(qi, 0, True)
    n_pairs = qi // 2

    def pair(i, carry):
        j = 2 * i
        produce(j, 1, False)
        consume(jnp.where(i == 0, qi, j - 1), 0)
        produce(j + 1, 0, False)
        consume(j, 1)
        return carry

    lax.fori_loop(0, n_pairs, pair, 0)
    pending = jnp.where(n_pairs == 0, qi, 2 * n_pairs - 1)

    @pl.when(qi % 2 == 1)
    def _():
        produce(qi - 1, 1, False)
        consume(pending, 0)
        consume(qi - 1, 1)

    @pl.when(qi % 2 == 0)
    def _():
        consume(pending, 0)

    acc = acc_ref[...]
    o_ref[...] = (acc[0:MLA_V, :] / acc[MLA_V:MLA_V + 1, :]).astype(BF16)


def _flash(q, k, vt, b, l):
    bq, bk = ATTN_BLOCK_Q, ATTN_BLOCK_K
    assert bq == bk
    nq = l // bq
    nk = l // bk
    q3 = q.reshape(b, l, MLA_HEADS * MLA_HEAD_PAD)
    k3 = k.reshape(b, l, MLA_HEADS * MLA_HEAD_PAD)
    return pl.pallas_call(
        _flash_kernel,
        out_shape=jax.ShapeDtypeStruct((b, MLA_HEADS * MLA_V, l), BF16),
        grid=(b, MLA_HEADS, nq),
        in_specs=[pl.BlockSpec((None, bq, MLA_HEAD_PAD), lambda bi, h, qi: (bi, qi, h)),
                  pl.BlockSpec((None, l, MLA_HEAD_PAD), lambda bi, h, qi: (bi, 0, h)),
                  pl.BlockSpec((None, nk, None, VT_ROWS, bk), lambda bi, h, qi: (bi, 0, h, 0, 0))],
        out_specs=pl.BlockSpec((None, MLA_V, bq), lambda bi, h, qi: (bi, h, qi)),
        scratch_shapes=[pltpu.VMEM((1, bq), F32), pltpu.VMEM((VT_ROWS, bq), F32),
                        pltpu.VMEM((bk, bq), F32), pltpu.VMEM((bk, bq), F32),
                        pltpu.VMEM((1, bq), F32), pltpu.VMEM((1, bq), F32)],
        compiler_params=_params("arbitrary", "arbitrary", "arbitrary"),
        name="mla_flash",
    )(q3, k3, vt)


def _merge_kernel(x_ref, yssd_ref, yret_ref, ymt_ref, ygla_ref, gpre_ref, wg_ref, bg_ref, wb_ref, wo_ref,
                  gpost_ref, o_ref):
    x = x_ref[...]
    h = _rms(x, gpre_ref[...]).astype(BF16)
    branches = (_dot(yssd_ref[...], wb_ref[0]), _dot(yret_ref[...], wb_ref[1]),
                _dot_tn(ymt_ref[...], wb_ref[2]), _dot(ygla_ref[...], wb_ref[3]))
    merged = None
    for i, br in enumerate(branches):
        gate = jax.nn.sigmoid(_dot(h, wg_ref[:, i * D_MODEL:(i + 1) * D_MODEL]) + bg_ref[:, i * D_MODEL:(i + 1) * D_MODEL])
        merged = gate * br if merged is None else merged + gate * br
    o = _dot(merged.astype(BF16), wo_ref[...])
    o_ref[...] = x + _rms(o, gpost_ref[...])


def _merge(x3, y_ssd, y_ret, y_mla_t, y_gla, g_pre, wg, b_gate, wb, wo, g_post):
    b, l, _ = x3.shape
    tm = min(TOKEN_TILE, l)
    nt = l // tm
    w = BRANCH_WIDTH
    tok = lambda width: pl.BlockSpec((None, tm, width), lambda bi, j: (bi, j, 0))
    return pl.pallas_call(
        _merge_kernel,
        out_shape=jax.ShapeDtypeStruct((b, l, D_MODEL), F32),
        grid=(b, nt),
        in_specs=[tok(D_MODEL), tok(w), tok(w),
                  pl.BlockSpec((None, w, tm), lambda bi, j: (bi, 0, j)), tok(w),
                  _const_spec((1, D_MODEL)), _const_spec(wg.shape), _const_spec((1, N_BRANCHES * D_MODEL)),
                  _const_spec(wb.shape), _const_spec(wo.shape), _const_spec((1, D_MODEL))],
        out_specs=tok(D_MODEL),
        compiler_params=_params("arbitrary", "arbitrary"),
        name="merge_out",
    )(x3, y_ssd.reshape(b, l, w), y_ret.reshape(b, l, w), y_mla_t, y_gla.reshape(b, l, w),
      g_pre, wg, b_gate.reshape(1, -1), wb, wo, g_post)


def _mlp_kernel(x_ref, gpre_ref, wi_ref, wo_ref, gpost_ref, o_ref):
    x = x_ref[...]
    h = _rms(x, gpre_ref[...]).astype(BF16)
    acc = None
    for j in range(D_FF // D_MODEL):
        u = jnp.maximum(_dot(h, wi_ref[:, j * D_MODEL:(j + 1) * D_MODEL]), 0.0)
        part = _dot((u * u).astype(BF16), wo_ref[j * D_MODEL:(j + 1) * D_MODEL, :])
        acc = part if acc is None else acc + part
    o_ref[...] = x + _rms(acc, gpost_ref[...])


def _mlp(x2, g_pre, wi, wo, g_post):
    t = x2.shape[0]
    tm = min(TOKEN_TILE, t)
    return pl.pallas_call(
        _mlp_kernel,
        out_shape=jax.ShapeDtypeStruct((t, D_MODEL), F32),
        grid=(t // tm,),
        in_specs=[pl.BlockSpec((tm, D_MODEL), lambda i: (i, 0)), _const_spec((1, D_MODEL)),
                  _const_spec(wi.shape), _const_spec(wo.shape), _const_spec((1, D_MODEL))],
        out_specs=pl.BlockSpec((tm, D_MODEL), lambda i: (i, 0)),
        compiler_params=_params("arbitrary"),
        name="mlp",
    )(x2, g_pre, wi, wo, g_post)


def _rot_perm(n_heads, dim):
    half = dim // 2
    return np.concatenate([h * dim + (np.arange(dim) + half) % dim for h in range(n_heads)])


def _layer_weights(w_in, mla_w_uq, mla_w_ukv):
    col = lambda a, b: w_in[:, a:b]
    w_ssd = col(_O_Z, _O_DT).astype(BF16)
    rq, rk = col(_O_RQ, _O_RK), col(_O_RK, _O_RV)
    perm = _rot_perm(RET_HEADS, RET_QK_DIM)
    w_ret = jnp.concatenate([rq, rk, rq[:, perm], rk[:, perm], col(_O_RV, _O_CQ)], axis=1).astype(BF16)
    kr = col(_O_KR, _O_GQ)
    zl = jnp.zeros((D_MODEL, MLA_PE_LANE), F32)
    zr = jnp.zeros((D_MODEL, LANES - MLA_PE_LANE - MLA_ROPE), F32)
    kperm = _rot_perm(1, MLA_ROPE)
    w_mla = jnp.concatenate([col(_O_CQ, _O_KR), zl, kr, zr, zl, kr[:, kperm], zr], axis=1).astype(BF16)
    w_gla = jnp.concatenate([col(_O_GQ, _O_GGK), col(_O_GG, _O_GATE)], axis=1).astype(BF16)
    w_small = jnp.concatenate(
        [col(_O_DT, _O_RQ), col(_O_GGK, _O_GG),
         jnp.zeros((D_MODEL, LANES - SSD_HEADS - GLA_GATE_RANK), F32)], axis=1).astype(BF16)
    w_gate = col(_O_GATE, _O_END).astype(BF16)

    hd = MLA_NOPE + MLA_ROPE
    uq = mla_w_uq.reshape(MLA_Q_LORA, MLA_HEADS, hd)
    padq = jnp.zeros((MLA_Q_LORA, MLA_HEADS, MLA_HEAD_PAD - hd), F32)
    wq = jnp.concatenate([uq, padq], axis=2).reshape(MLA_Q_LORA, -1).astype(BF16)
    uq_rot = uq[:, :, MLA_NOPE + kperm]
    wqr = jnp.concatenate([jnp.zeros((MLA_Q_LORA, MLA_HEADS, MLA_NOPE), F32), uq_rot, padq],
                          axis=2).reshape(MLA_Q_LORA, -1).astype(BF16)
    ukv = mla_w_ukv.reshape(MLA_KV_LORA, MLA_HEADS, MLA_NOPE + MLA_V)
    wk = jnp.concatenate([ukv[:, :, :MLA_NOPE],
                          jnp.zeros((MLA_KV_LORA, MLA_HEADS, MLA_HEAD_PAD - MLA_NOPE), F32)],
                         axis=2).reshape(MLA_KV_LORA, -1).astype(BF16)
    wvt = ukv[:, :, MLA_NOPE:].reshape(MLA_KV_LORA, -1).T.astype(BF16)
    return w_ssd, w_ret, w_mla, w_gla, w_small, w_gate, wq, wqr, wk, wvt


def _layer(x3, tabs, w_in, b_gate, ssd_conv_w, ssd_conv_b, ssd_dt_bias, ssd_a_log, ssd_d, ssd_norm, ret_norm,
           mla_q_norm, mla_w_uq, mla_kv_norm, mla_w_ukv, gla_w_gk2, gla_b_gk, gla_norm, w_branch, w_out,
           norm_pre_mix, norm_post_mix, norm_pre_mlp, norm_post_mlp, w_mlp_in, w_mlp_out):
    b, l, d = x3.shape
    c_ret, s_ret, c_mla, s_mla = tabs
    w_ssd, w_ret, w_mla, w_gla, w_small, w_gate, wq, wqr, wk, wvt = _layer_weights(w_in, mla_w_uq, mla_w_ukv)
    g_pre = norm_pre_mix.reshape(1, -1)
    x2 = x3.reshape(b * l, d)
    o_ssd, o_ret, o_mla, o_gla, o_small = _inproj(x2, g_pre, w_ssd, w_ret, w_mla, w_gla, w_small)
    y_ssd = _ssd(o_ssd, o_small, ssd_conv_w, ssd_conv_b, ssd_dt_bias, ssd_a_log, ssd_d, ssd_norm, b, l)
    y_ret = _ret(o_ret, c_ret, s_ret, ret_norm, b, l)
    q, k, vt = _mla_proj(o_mla, c_mla, s_mla, mla_q_norm, mla_kv_norm, wq, wqr, wk, wvt, b, l)
    y_mla_t = _flash(q, k, vt, b, l)
    y_gla = _gla(o_gla, o_small, gla_w_gk2, gla_b_gk, gla_norm, b, l)
    x3 = _merge(x3, y_ssd, y_ret, y_mla_t, y_gla, g_pre, w_gate, b_gate, w_branch.astype(BF16),
                w_out.astype(BF16), norm_post_mix.reshape(1, -1))
    x2 = _mlp(x3.reshape(b * l, d), norm_pre_mlp.reshape(1, -1), w_mlp_in.astype(BF16),
              w_mlp_out.astype(BF16), norm_post_mlp.reshape(1, -1))
    return x2.reshape(b, l, d)


def kernel(x, positions, w_in, b_gate, ssd_conv_w, ssd_conv_b, ssd_dt_bias, ssd_a_log, ssd_d, ssd_norm, ret_norm, mla_q_norm, mla_w_uq, mla_kv_norm, mla_w_ukv, gla_w_gk2, gla_b_gk, gla_norm, w_branch, w_out, norm_pre_mix, norm_post_mix, norm_pre_mlp, norm_post_mlp, w_mlp_in, w_mlp_out):
    per_layer = (w_in, b_gate, ssd_conv_w, ssd_conv_b, ssd_dt_bias, ssd_a_log, ssd_d, ssd_norm, ret_norm,
                 mla_q_norm, mla_w_uq, mla_kv_norm, mla_w_ukv, gla_w_gk2, gla_b_gk, gla_norm, w_branch, w_out,
                 norm_pre_mix, norm_post_mix, norm_pre_mlp, norm_post_mlp, w_mlp_in, w_mlp_out)
    tabs = _rope_tables(positions)
    for i in range(w_in.shape[0]):
        x = _layer(x, tabs, *(p[i] for p in per_layer))
    return x
```

```python
import functools

import numpy as np
import jax
import jax.numpy as jnp
from jax import lax
from jax.experimental import pallas as pl
from jax.experimental.pallas import tpu as pltpu

F32 = jnp.float32
BF16 = jnp.bfloat16

D_MODEL = 1024
SSD_HEADS = 8
SSD_HEAD_DIM = 64
SSD_INNER = 512
SSD_GROUPS = 2
SSD_STATE = 128
SSD_CONV = 4
SSD_CHUNK = 128
SSD_XBC = 1024
RET_HEADS = 4
RET_QK_DIM = 64
RET_V_DIM = 128
RET_CHUNK = 128
MLA_HEADS = 8
MLA_Q_LORA = 256
MLA_KV_LORA = 128
MLA_NOPE = 64
MLA_ROPE = 32
MLA_V = 64
GLA_HEADS = 4
GLA_K_DIM = 64
GLA_V_DIM = 128
GLA_GATE_RANK = 16
GLA_GATE_NORM = 16.0
GLA_CHUNK = 64
N_BRANCHES = 4
BRANCH_WIDTH = 512
D_FF = 4 * D_MODEL
ROPE_THETA = 10000.0
RMS_EPS = 1e-6

LANES = 128
MLA_HEAD_PAD = 128
VT_ROWS = MLA_V + 16
LOG2_E = 1.4426950408889634
VMEM_LIMIT = 56 * 1024 * 1024

TOKEN_TILE = 512
ATTN_BLOCK_Q = 512
ATTN_BLOCK_K = 512
GLA_STEP = 256
RET_STEP = 256
SSD_STEP = 256

_SIZES = (SSD_INNER, SSD_XBC, SSD_HEADS,
          RET_HEADS * RET_QK_DIM, RET_HEADS * RET_QK_DIM, RET_HEADS * RET_V_DIM, RET_HEADS * RET_V_DIM,
          MLA_Q_LORA, MLA_KV_LORA, MLA_ROPE,
          GLA_HEADS * GLA_K_DIM, GLA_HEADS * GLA_K_DIM, GLA_HEADS * GLA_V_DIM, GLA_GATE_RANK,
          GLA_HEADS * GLA_V_DIM, N_BRANCHES * D_MODEL)
_OFFS = tuple(int(v) for v in np.concatenate([[0], np.cumsum(_SIZES)]))
(_O_Z, _O_XBC, _O_DT, _O_RQ, _O_RK, _O_RV, _O_RG, _O_CQ, _O_CKV, _O_KR,
 _O_GQ, _O_GK, _O_GV, _O_GGK, _O_GG, _O_GATE, _O_END) = _OFFS

SMALL_DT_LANE = 0
SMALL_GGK_LANE = 8
MLA_PE_LANE = MLA_NOPE


def _dot(a, b, precision=None):
    return jnp.dot(a, b, preferred_element_type=F32, precision=precision)


def _dot_nt(a, b):
    return lax.dot_general(a, b, (((1,), (1,)), ((), ())), preferred_element_type=F32)


def _dot_tn(a, b):
    return lax.dot_general(a, b, (((0,), (0,)), ((), ())), preferred_element_type=F32)


def _rms(x, g):
    return x * lax.rsqrt(jnp.mean(x * x, axis=-1, keepdims=True) + RMS_EPS) * g


def _silu(x):
    return x * jax.nn.sigmoid(x)


def _softplus(x):
    return jnp.maximum(x, 0.0) + jnp.log1p(jnp.exp(-jnp.abs(x)))


def _params(*sem):
    return pltpu.CompilerParams(dimension_semantics=sem, vmem_limit_bytes=VMEM_LIMIT)


def _const_spec(shape):
    nd = len(shape)
    return pl.BlockSpec(shape, lambda *_: (0,) * nd, pipeline_mode=pl.Buffered(1))


def _tril(n):
    r = lax.broadcasted_iota(jnp.int32, (n, n), 0)
    c = lax.broadcasted_iota(jnp.int32, (n, n), 1)
    return r >= c


def _rope_kernel(pos_ref, c_ref, cr_ref, sr_ref, cm_ref, sm_ref):
    pos = pos_ref[...]
    c = c_ref[...]
    ang_r = pos * c[0:1, :]
    cr_ref[...] = jnp.cos(ang_r)
    sr_ref[...] = jnp.sin(ang_r) * c[1:2, :]
    ang_m = pos * c[2:3, :]
    cm_ref[...] = jnp.cos(ang_m)
    sm_ref[...] = jnp.sin(ang_m) * c[3:4, :]


def _rope_consts():
    lane = np.arange(LANES)
    half_r = RET_QK_DIM // 2
    half_m = MLA_ROPE // 2
    inv_r = ROPE_THETA ** (-jnp.arange(half_r, dtype=F32) / half_r)
    inv_m = ROPE_THETA ** (-jnp.arange(half_m, dtype=F32) / half_m)
    freq_r = inv_r[lane % half_r]
    sign_r = jnp.asarray(np.where((lane % RET_QK_DIM) < half_r, -1.0, 1.0), F32)
    in_pe = (lane >= MLA_PE_LANE) & (lane < MLA_PE_LANE + MLA_ROPE)
    freq_m = jnp.where(in_pe, inv_m[(lane - MLA_PE_LANE) % half_m], 0.0)
    sign_m = jnp.asarray(np.where(in_pe, np.where((lane - MLA_PE_LANE) < half_m, -1.0, 1.0), 0.0), F32)
    z = jnp.zeros((LANES,), F32)
    return jnp.stack([freq_r, sign_r, freq_m, sign_m, z, z, z, z]).astype(F32)


def _rope_tables(positions):
    b, l = positions.shape
    t = b * l
    tb = min(1024, t)
    pos = positions.astype(F32).reshape(t, 1)
    tab = jax.ShapeDtypeStruct((t, LANES), F32)
    spec = pl.BlockSpec((tb, LANES), lambda i: (i, 0))
    return pl.pallas_call(
        _rope_kernel,
        out_shape=(tab, tab, tab, tab),
        grid=(t // tb,),
        in_specs=[pl.BlockSpec((tb, 1), lambda i: (i, 0)), _const_spec((8, LANES))],
        out_specs=(spec, spec, spec, spec),
        compiler_params=_params("arbitrary"),
        name="rope_tables",
    )(pos, _rope_consts())


def _inproj_kernel(x_ref, g_ref, wssd_ref, wret_ref, wmla_ref, wgla_ref, wsm_ref,
                   ossd_ref, oret_ref, omla_ref, ogla_ref, osm_ref):
    h = _rms(x_ref[...], g_ref[...]).astype(BF16)
    ossd_ref[...] = _dot(h, wssd_ref[...]).astype(BF16)
    oret_ref[...] = _dot(h, wret_ref[...]).astype(BF16)
    omla_ref[...] = _dot(h, wmla_ref[...]).astype(BF16)
    ogla_ref[...] = _dot(h, wgla_ref[...]).astype(BF16)
    osm_ref[...] = _dot(h, wsm_ref[...])


def _inproj(x2, g, wssd, wret, wmla, wgla, wsm):
    t = x2.shape[0]
    tm = min(TOKEN_TILE, t)
    ws = (wssd, wret, wmla, wgla, wsm)
    outs = tuple(jax.ShapeDtypeStruct((t, w.shape[1]), BF16) for w in ws[:4]) + (
        jax.ShapeDtypeStruct((t, LANES), F32),)
    return pl.pallas_call(
        _inproj_kernel,
        out_shape=outs,
        grid=(t // tm,),
        in_specs=[pl.BlockSpec((tm, D_MODEL), lambda i: (i, 0)), _const_spec((1, D_MODEL))]
                 + [_const_spec(w.shape) for w in ws],
        out_specs=tuple(pl.BlockSpec((tm, o.shape[1]), lambda i: (i, 0)) for o in outs),
        compiler_params=_params("arbitrary"),
        name="in_proj",
    )(x2, g, *ws)


def _split3(x):
    hi = x.astype(BF16)
    r = x - hi.astype(F32)
    mid = r.astype(BF16)
    lo = (r - mid.astype(F32)).astype(BF16)
    return hi, mid, lo


def _dot_exact01(a01, x):
    n = x.shape[1]
    parts = _dot(a01, jnp.concatenate(_split3(x), axis=1))
    return parts[:, 0:n] + parts[:, n:2 * n] + parts[:, 2 * n:3 * n]


def _chunk_tril(rows, chunk):
    r = lax.broadcasted_iota(jnp.int32, (rows, rows), 0)
    c = lax.broadcasted_iota(jnp.int32, (rows, rows), 1)
    shift = chunk.bit_length() - 1
    return ((r >> shift) == (c >> shift)) & (r >= c)


def _ssd_kernel(blk_ref, sm_ref, cw_ref, cb_ref, shift_ref, dtb_ref, alog_ref, exp_ref, dsk_ref, ng_ref,
                y_ref, conv_ref, state_ref):
    q = SSD_CHUNK
    rows = blk_ref.shape[0]
    n_chunks = rows // q
    tail = 8

    @pl.when(pl.program_id(1) == 0)
    def _():
        conv_ref[0:tail, :] = jnp.zeros((tail, SSD_XBC), F32)
        state_ref[...] = jnp.zeros_like(state_ref)

    z = blk_ref[:, 0:SSD_INNER].astype(F32)
    xbc_b = blk_ref[:, SSD_INNER:SSD_INNER + SSD_XBC]
    xbc = xbc_b.astype(F32)
    conv_ref[tail:2 * tail, :] = xbc[0:tail, :]
    acc = cb_ref[...] + cw_ref[SSD_CONV - 1:SSD_CONV, :] * xbc
    head = acc[0:tail, :]
    for j in range(SSD_CONV - 1):
        off = tail - (SSD_CONV - 1) + j
        acc = acc + cw_ref[j:j + 1, :] * _dot(shift_ref[j], xbc_b)
        head = head + cw_ref[j:j + 1, :] * conv_ref[off:off + tail, :]
    conv_ref[0:tail, :] = xbc[rows - tail:rows, :]
    xbc = _silu(jnp.concatenate([head, acc[tail:, :]], axis=0))
    xs = xbc[:, 0:SSD_INNER]
    gn = SSD_GROUPS * SSD_STATE
    bm = xbc[:, SSD_INNER:SSD_INNER + gn].astype(BF16)
    cm = xbc[:, SSD_INNER + gn:SSD_INNER + 2 * gn].astype(BF16)

    dt = _softplus(sm_ref[...] + dtb_ref[...])
    la = dt * (-jnp.exp(alog_ref[...]))
    tri = jnp.where(_chunk_tril(rows, q), 1.0, 0.0).astype(BF16)
    cum = _dot_exact01(tri, la)
    expand3 = exp_ref[...]
    cum_ch = _dot(jnp.concatenate(_split3(cum), axis=1), expand3)
    dt_ch = _dot(jnp.concatenate(_split3(dt), axis=1), expand3)
    xdt = xs * dt_ch
    xdt_b = xdt.astype(BF16)
    last = jnp.concatenate(
        [jnp.broadcast_to(cum_ch[(j + 1) * q - 1:(j + 1) * q, :], (q, SSD_INNER)) for j in range(n_chunks)], axis=0)
    xdte_b = (xdt * jnp.exp(last - cum_ch)).astype(BF16)

    hpg = SSD_HEADS // SSD_GROUPS
    gw = hpg * SSD_HEAD_DIM
    lane_lo = lax.broadcasted_iota(jnp.int32, (q, LANES), 1) < SSD_HEAD_DIM
    mask = _tril(q)
    state = [state_ref[g] for g in range(SSD_GROUPS)]
    y_rows = []
    for c in range(n_chunks):
        sl = slice(c * q, (c + 1) * q)
        cum_c = cum[sl]
        cum_t = cum_c.T
        chunk_decay = jnp.exp(cum_ch[(c + 1) * q - 1:(c + 1) * q, :])
        ydiag, yoff = [], []
        for g in range(SSD_GROUPS):
            cg = cm[sl, g * SSD_STATE:(g + 1) * SSD_STATE]
            bg = bm[sl, g * SSD_STATE:(g + 1) * SSD_STATE]
            s = _dot_nt(cg, bg)
            yoff.append(_dot(cg, state[g].astype(BF16)))
            for j in range(hpg // 2):
                xpair = xdt_b[sl, g * gw + j * LANES:g * gw + (j + 1) * LANES]
                ys = []
                for e in range(2):
                    h = g * hpg + 2 * j + e
                    seg = cum_c[:, h:h + 1] - cum_t[h:h + 1, :]
                    a = (s * jnp.exp(jnp.where(mask, seg, -jnp.inf))).astype(BF16)
                    ys.append(_dot(a, xpair))
                ydiag.append(jnp.where(lane_lo, ys[0], ys[1]))
            state[g] = (state[g] * chunk_decay[:, g * gw:(g + 1) * gw]
                        + _dot_tn(bg, xdte_b[sl, g * gw:(g + 1) * gw]))
        y_rows.append(jnp.concatenate(ydiag, axis=1) + jnp.concatenate(yoff, axis=1) * jnp.exp(cum_ch[sl]))
    for g in range(SSD_GROUPS):
        state_ref[g] = state[g]
    y = jnp.concatenate(y_rows, axis=0)
    y = (y + dsk_ref[...] * xs) * _silu(z)
    outs = []
    for g in range(SSD_GROUPS):
        outs.append(_rms(y[:, g * gw:(g + 1) * gw], ng_ref[:, g * gw:(g + 1) * gw]))
    y_ref[...] = jnp.concatenate(outs, axis=1).astype(BF16)


def _ssd(o_ssd, o_small, conv_w, conv_b, dt_bias, a_log, d_skip, norm_g, b, l):
    rows = min(SSD_STEP, l)
    nc = l // rows
    pad = LANES - SSD_HEADS
    dtb = jnp.pad(dt_bias, (0, pad)).reshape(1, LANES)
    alog = jnp.pad(a_log, (0, pad)).reshape(1, LANES)
    head_of = np.arange(SSD_INNER) // SSD_HEAD_DIM
    expand = np.arange(LANES)[:, None] == head_of[None, :]
    expand3 = jnp.asarray(np.concatenate([expand] * 3, axis=0), BF16)
    dsk = jnp.repeat(d_skip, SSD_HEAD_DIM).reshape(1, SSD_INNER)
    t = np.arange(rows)
    shifts = jnp.asarray(np.stack([t[None, :] == t[:, None] - (SSD_CONV - 1 - j) for j in range(SSD_CONV - 1)]), BF16)
    row = lambda bi, ci: (bi * nc + ci, 0)
    return pl.pallas_call(
        _ssd_kernel,
        out_shape=jax.ShapeDtypeStruct((b * l, SSD_INNER), BF16),
        grid=(b, nc),
        in_specs=[pl.BlockSpec((rows, SSD_INNER + SSD_XBC), row), pl.BlockSpec((rows, LANES), row),
                  _const_spec((SSD_CONV, SSD_XBC)), _const_spec((1, SSD_XBC)), _const_spec(shifts.shape),
                  _const_spec((1, LANES)), _const_spec((1, LANES)), _const_spec(expand3.shape),
                  _const_spec((1, SSD_INNER)), _const_spec((1, SSD_INNER))],
        out_specs=pl.BlockSpec((rows, SSD_INNER), row),
        scratch_shapes=[pltpu.VMEM((16, SSD_XBC), F32),
                        pltpu.VMEM((SSD_GROUPS, SSD_STATE, SSD_INNER // SSD_GROUPS), F32)],
        compiler_params=_params("arbitrary", "arbitrary"),
        name="ssd_mixer",
    )(o_ssd, o_small, conv_w, conv_b.reshape(1, -1), shifts, dtb, alog, expand3, dsk, norm_g.reshape(1, -1))


def _head_rows(full, n_heads, rows, cols):
    return jnp.concatenate([full[h * rows:(h + 1) * rows, h * cols:(h + 1) * cols] for h in range(n_heads)], axis=0)


def _ret_kernel(blk_ref, c_ref, s_ref, dm_ref, qs_ref, ks_ref, cd_ref, ng_ref, y_ref, state_ref):
    @pl.when(pl.program_id(1) == 0)
    def _():
        state_ref[...] = jnp.zeros_like(state_ref)

    w = RET_HEADS * RET_QK_DIM
    vw = RET_HEADS * RET_V_DIM
    ct = jnp.concatenate([c_ref[...]] * (w // LANES), axis=1)
    st = jnp.concatenate([s_ref[...]] * (w // LANES), axis=1)
    q = blk_ref[:, 0:w].astype(F32) * ct + blk_ref[:, 2 * w:3 * w].astype(F32) * st
    k = (blk_ref[:, w:2 * w].astype(F32) * ct + blk_ref[:, 3 * w:4 * w].astype(F32) * st) * (RET_QK_DIM ** -0.5)
    v = blk_ref[:, 4 * w:4 * w + vw]
    g = blk_ref[:, 4 * w + vw:4 * w + 2 * vw].astype(F32)
    k_b = k.astype(BF16)
    qo = q * qs_ref[...]
    lane = lax.broadcasted_iota(jnp.int32, (1, w), 1)
    state = state_ref[...]
    state_b = state.astype(BF16)
    ys = []
    for h in range(RET_HEADS):
        in_head = (lane >= h * RET_QK_DIM) & (lane < (h + 1) * RET_QK_DIM)
        qh = jnp.where(in_head, q, 0.0).astype(BF16)
        sc = (_dot_nt(qh, k_b) * dm_ref[h]).astype(BF16)
        yh = _dot(sc, v[:, h * RET_V_DIM:(h + 1) * RET_V_DIM]) + _dot(jnp.where(in_head, qo, 0.0).astype(BF16), state_b)
        ys.append(_rms(yh, ng_ref[:, h * RET_V_DIM:(h + 1) * RET_V_DIM]))
    upd = _head_rows(_dot_tn((k * ks_ref[...]).astype(BF16), v), RET_HEADS, RET_QK_DIM, RET_V_DIM)
    state_ref[...] = state * cd_ref[...] + upd
    y_ref[...] = (jnp.concatenate(ys, axis=1) * _silu(g)).astype(BF16)


def _ret_consts(rows):
    lg = np.log1p(-np.exp2(-5.0 - np.arange(RET_HEADS, dtype=np.float64)))
    i = np.arange(rows)
    dm = np.where(i[:, None] >= i[None, :], np.exp(lg[:, None, None] * (i[:, None] - i[None, :])[None]), 0.0)
    qs = np.repeat(np.exp(lg[None, :] * (i[:, None] + 1)), RET_QK_DIM, axis=1)
    ks = np.repeat(np.exp(lg[None, :] * (rows - 1 - i[:, None])), RET_QK_DIM, axis=1)
    cd = np.broadcast_to(np.repeat(np.exp(lg * rows), RET_QK_DIM)[:, None], (RET_HEADS * RET_QK_DIM, RET_V_DIM))
    return tuple(jnp.asarray(a, F32) for a in (dm, qs, ks, cd))


def _ret(o_ret, c_tab, s_tab, norm_g, b, l):
    rows = min(RET_STEP, l)
    nc = l // rows
    w = RET_HEADS * RET_QK_DIM
    vw = RET_HEADS * RET_V_DIM
    dm, qs, ks, cd = _ret_consts(rows)
    row = lambda bi, ci: (bi * nc + ci, 0)
    return pl.pallas_call(
        _ret_kernel,
        out_shape=jax.ShapeDtypeStruct((b * l, vw), BF16),
        grid=(b, nc),
        in_specs=[pl.BlockSpec((rows, 4 * w + 2 * vw), row), pl.BlockSpec((rows, LANES), row),
                  pl.BlockSpec((rows, LANES), row),
                  _const_spec(dm.shape), _const_spec(qs.shape), _const_spec(ks.shape),
                  _const_spec(cd.shape), _const_spec((1, vw))],
        out_specs=pl.BlockSpec((rows, vw), row),
        scratch_shapes=[pltpu.VMEM((w, RET_V_DIM), F32)],
        compiler_params=_params("arbitrary", "arbitrary"),
        name="ret_mixer",
    )(o_ret, c_tab, s_tab, dm, qs, ks, cd, norm_g.reshape(1, -1))


def _gla_kernel(blk_ref, sm_ref, w2_ref, bgk_ref, ng_ref, y_ref, state_ref):
    @pl.when(pl.program_id(1) == 0)
    def _():
        state_ref[...] = jnp.zeros_like(state_ref)

    c = GLA_CHUNK
    rows = blk_ref.shape[0]
    kw = GLA_HEADS * GLA_K_DIM
    vw = GLA_HEADS * GLA_V_DIM
    sm_hi, sm_mid, _ = _split3(sm_ref[...])
    logits = _dot(jnp.concatenate([sm_hi, sm_hi, sm_mid], axis=1), w2_ref[...]) + bgk_ref[...]
    log_g = -_softplus(-logits) * (1.0 / GLA_GATE_NORM)
    tri = _chunk_tril(rows, c)
    cum = _dot_exact01(jnp.where(tri, 1.0, 0.0).astype(BF16), log_g)
    n_chunks = rows // c
    total = jnp.concatenate(
        [jnp.broadcast_to(cum[(j + 1) * c - 1:(j + 1) * c, :], (c, kw)) for j in range(n_chunks)], axis=0)
    q = blk_ref[:, 0:kw].astype(F32) * (GLA_K_DIM ** -0.5)
    k = blk_ref[:, kw:2 * kw].astype(F32)
    v = blk_ref[:, 2 * kw:2 * kw + vw]
    g = blk_ref[:, 2 * kw + vw:2 * kw + 2 * vw].astype(F32)
    q_in = q * jnp.exp(cum)
    k_in = (k * jnp.exp(-cum)).astype(BF16)
    k_st = (k * jnp.exp(total - cum)).astype(BF16)
    lane = lax.broadcasted_iota(jnp.int32, (1, kw), 1)
    in_head = [(lane >= h * GLA_K_DIM) & (lane < (h + 1) * GLA_K_DIM) for h in range(GLA_HEADS)]

    state = state_ref[...]
    o_inter = []
    for j in range(n_chunks):
        sl = slice(j * c, (j + 1) * c)
        q_stack = jnp.concatenate([jnp.where(m, q_in[sl], 0.0) for m in in_head], axis=0).astype(BF16)
        k_stack = jnp.concatenate([jnp.where(m, k_st[sl], 0) for m in in_head], axis=0)
        v_stack = jnp.concatenate([v[sl, h * GLA_V_DIM:(h + 1) * GLA_V_DIM] for h in range(GLA_HEADS)], axis=0)
        oi = _dot(q_stack, state.astype(BF16))
        o_inter.append(jnp.concatenate([oi[h * c:(h + 1) * c] for h in range(GLA_HEADS)], axis=1))
        last = cum[(j + 1) * c - 1:(j + 1) * c, :]
        decay = jnp.exp(jnp.broadcast_to(last, (LANES, kw)).T)
        state = state * decay + _dot_tn(k_stack, v_stack)
    state_ref[...] = state
    o_inter = jnp.concatenate(o_inter, axis=0)
    outs = []
    for h in range(GLA_HEADS):
        qh = jnp.where(in_head[h], q_in, 0.0).astype(BF16)
        sc = jnp.where(tri, _dot_nt(qh, k_in), 0.0).astype(BF16)
        oh = _dot(sc, v[:, h * GLA_V_DIM:(h + 1) * GLA_V_DIM]) + o_inter[:, h * GLA_V_DIM:(h + 1) * GLA_V_DIM]
        outs.append(_rms(oh, ng_ref[:, h * GLA_V_DIM:(h + 1) * GLA_V_DIM]))
    y_ref[...] = (jnp.concatenate(outs, axis=1) * _silu(g)).astype(BF16)


def _gla(o_gla, o_small, w_gk2, b_gk, norm_g, b, l):
    rows = min(GLA_STEP, l)
    ns = l // rows
    kw = GLA_HEADS * GLA_K_DIM
    vw = GLA_HEADS * GLA_V_DIM
    w2 = jnp.zeros((LANES, kw), F32).at[SMALL_GGK_LANE:SMALL_GGK_LANE + GLA_GATE_RANK].set(w_gk2)
    w2_hi = w2.astype(BF16)
    w2_mid = (w2 - w2_hi.astype(F32)).astype(BF16)
    w2 = jnp.concatenate([w2_hi, w2_mid, w2_hi], axis=0)
    row = lambda bi, ci: (bi * ns + ci, 0)
    return pl.pallas_call(
        _gla_kernel,
        out_shape=jax.ShapeDtypeStruct((b * l, vw), BF16),
        grid=(b, ns),
        in_specs=[pl.BlockSpec((rows, 2 * kw + 2 * vw), row), pl.BlockSpec((rows, LANES), row),
                  _const_spec(w2.shape), _const_spec((1, kw)), _const_spec((1, vw))],
        out_specs=pl.BlockSpec((rows, vw), row),
        scratch_shapes=[pltpu.VMEM((kw, GLA_V_DIM), F32)],
        compiler_params=_params("arbitrary", "arbitrary"),
        name="gla_mixer",
    )(o_gla, o_small, w2, b_gk.reshape(1, -1), norm_g.reshape(1, -1))


def _mla_proj_kernel(blk_ref, c_ref, s_ref, qn_ref, kvn_ref, wq_ref, wqr_ref, wk_ref, wvt_ref,
                     q_ref, k_ref, vt_ref):
    o_ckv = MLA_Q_LORA
    o_kr = o_ckv + MLA_KV_LORA
    cq = _rms(blk_ref[:, 0:MLA_Q_LORA].astype(F32), qn_ref[...]).astype(BF16)
    ckv = _rms(blk_ref[:, o_ckv:o_kr].astype(F32), kvn_ref[...]).astype(BF16)
    c = c_ref[...]
    s = s_ref[...]
    ct = jnp.concatenate([c] * MLA_HEADS, axis=1)
    st = jnp.concatenate([s] * MLA_HEADS, axis=1)
    scale = (MLA_NOPE + MLA_ROPE) ** -0.5 * LOG2_E
    q = (_dot(cq, wq_ref[...]) * ct + _dot(cq, wqr_ref[...]) * st) * scale
    q_ref[...] = q.astype(BF16)
    kpe = blk_ref[:, o_kr:o_kr + LANES].astype(F32) * c + blk_ref[:, o_kr + LANES:o_kr + 2 * LANES].astype(F32) * s
    k = _dot(ckv, wk_ref[...]) + jnp.concatenate([kpe] * MLA_HEADS, axis=1)
    k_ref[...] = k.astype(BF16)
    vt = _dot_nt(wvt_ref[...], ckv).astype(BF16)
    ones = jnp.ones((VT_ROWS - MLA_V, vt.shape[1]), BF16)
    for h in range(MLA_HEADS):
        vt_ref[h, 0:MLA_V, :] = vt[h * MLA_V:(h + 1) * MLA_V, :]
        vt_ref[h, MLA_V:VT_ROWS, :] = ones


def _mla_proj(o_mla, c_tab, s_tab, q_norm, kv_norm, wq, wqr, wk, wvt, b, l):
    tm = min(ATTN_BLOCK_K, l)
    nk = l // tm
    hw = MLA_HEADS * MLA_HEAD_PAD
    row = lambda bi, j: (bi * nk + j, 0)
    return pl.pallas_call(
        _mla_proj_kernel,
        out_shape=(jax.ShapeDtypeStruct((b * l, hw), BF16), jax.ShapeDtypeStruct((b * l, hw), BF16),
                   jax.ShapeDtypeStruct((b, nk, MLA_HEADS, VT_ROWS, tm), BF16)),
        grid=(b, nk),
        in_specs=[pl.BlockSpec((tm, o_mla.shape[1]), row), pl.BlockSpec((tm, LANES), row),
                  pl.BlockSpec((tm, LANES), row),
                  _const_spec((1, MLA_Q_LORA)), _const_spec((1, MLA_KV_LORA)),
                  _const_spec(wq.shape), _const_spec(wqr.shape), _const_spec(wk.shape), _const_spec(wvt.shape)],
        out_specs=(pl.BlockSpec((tm, hw), row), pl.BlockSpec((tm, hw), row),
                   pl.BlockSpec((None, None, MLA_HEADS, VT_ROWS, tm), lambda bi, j: (bi, j, 0, 0, 0))),
        compiler_params=_params("arbitrary", "arbitrary"),
        name="mla_proj",
    )(o_mla, c_tab, s_tab, q_norm.reshape(1, -1), kv_norm.reshape(1, -1), wq, wqr, wk, wvt)


def _flash_kernel(nq, tq_ref, tk_ref, q_ref, k_ref, vt_ref, o_ref, m_ref, acc_ref, s0_ref, s1_ref, mb0_ref, mb1_ref):
    bq, bk = ATTN_BLOCK_Q, ATTN_BLOCK_K
    n_off = nq * (nq - 1) // 2
    n_blk = n_off + nq
    m_ref[...] = jnp.full(m_ref.shape, -jnp.inf, F32)
    acc_ref[...] = jnp.zeros(acc_ref.shape, F32)
    slots = ((s0_ref, mb0_ref), (s1_ref, mb1_ref))

    def produce(t, slot, diagonal):
        s_ref, mb_ref = slots[slot]
        q = q_ref[pl.ds(pl.multiple_of(tq_ref[t] * bq, bq), bq), :]
        k = k_ref[pl.ds(pl.multiple_of(tk_ref[t] * bk, bk), bk), :]
        s = _dot_nt(k, q)
        if diagonal:
            kpos = lax.broadcasted_iota(jnp.int32, (bk, bq), 0)
            qpos = lax.broadcasted_iota(jnp.int32, (bk, bq), 1)
            s = jnp.where(kpos <= qpos, s, -jnp.inf)
        s_ref[...] = s
        mb_ref[...] = jnp.max(s, axis=0, keepdims=True)

    def consume(t, slot, diagonal):
        s_ref, mb_ref = slots[slot]
        qi = tq_ref[t]
        m_old = m_ref[qi]
        m_new = jnp.maximum(m_old, mb_ref[...])
        alpha = jnp.exp2(m_old - m_new)
        p = jnp.exp2(s_ref[...] - m_new).astype(BF16)
        acc = acc_ref[qi] * alpha + _dot(vt_ref[tk_ref[t]], p)
        if diagonal:
            o_ref[qi] = (acc[0:MLA_V, :] / acc[MLA_V:MLA_V + 1, :]).astype(BF16)
        else:
            acc_ref[qi] = acc
            m_ref[qi] = m_new

    def static_steps(t_from, t_to):
        for t in range(t_from, t_to):
            if t + 1 < n_blk:
                produce(t + 1, (t + 1) % 2, t + 1 >= n_off)
            consume(t, t % 2, t >= n_off)

    def pair_loop(t_from, n_pairs, diagonal):
        assert t_from % 2 == 0

        def pair(i, carry):
            t0 = t_from + 2 * i
            produce(t0 + 1, 1, diagonal)
            consume(t0, 0, diagonal)
            produce(t0 + 2, 0, diagonal)
            consume(t0 + 1, 1, diagonal)
            return carry

        if n_pairs > 0:
            lax.fori_loop(0, n_pairs, pair, 0)

    produce(0, 0, n_off == 0)
    n_a = max(0, (n_off - 1) // 2)
    pair_loop(0, n_a, False)
    t_b = min(n_off + n_off % 2, n_blk)
    static_steps(2 * n_a, t_b)
    n_b = max(0, (n_blk - 1 - t_b) // 2)
    pair_loop(t_b, n_b, True)
    static_steps(t_b + 2 * n_b, n_blk)


def _flash_order(nq):
    off = [(qi, kb) for qi in range(nq) for kb in range(qi)]
    blocks = off + [(qi, qi) for qi in range(nq)]
    return (jnp.asarray([b[0] for b in blocks], jnp.int32), jnp.asarray([b[1] for b in blocks], jnp.int32))


def _flash(q, k, vt, b, l):
    bq, bk = ATTN_BLOCK_Q, ATTN_BLOCK_K
    assert bq == bk
    nq = l // bq
    nk = l // bk
    q3 = q.reshape(b, l, MLA_HEADS * MLA_HEAD_PAD)
    k3 = k.reshape(b, l, MLA_HEADS * MLA_HEAD_PAD)
    tq, tk = _flash_order(nq)
    grid_spec = pltpu.PrefetchScalarGridSpec(
        num_scalar_prefetch=2,
        grid=(b, MLA_HEADS),
        in_specs=[pl.BlockSpec((None, l, MLA_HEAD_PAD), lambda bi, h, *_: (bi, 0, h)),
                  pl.BlockSpec((None, l, MLA_HEAD_PAD), lambda bi, h, *_: (bi, 0, h)),
                  pl.BlockSpec((None, nk, None, VT_ROWS, bk), lambda bi, h, *_: (bi, 0, h, 0, 0))],
        out_specs=pl.BlockSpec((None, None, nq, MLA_V, bq), lambda bi, h, *_: (bi, h, 0, 0, 0)),
        scratch_shapes=[pltpu.VMEM((nq, 1, bq), F32), pltpu.VMEM((nq, VT_ROWS, bq), F32),
                        pltpu.VMEM((bk, bq), F32), pltpu.VMEM((bk, bq), F32),
                        pltpu.VMEM((1, bq), F32), pltpu.VMEM((1, bq), F32)])
    return pl.pallas_call(
        functools.partial(_flash_kernel, nq),
        out_shape=jax.ShapeDtypeStruct((b, MLA_HEADS, nq, MLA_V, bq), BF16),
        grid_spec=grid_spec,
        compiler_params=_params("arbitrary", "arbitrary"),
        name="mla_flash",
    )(tq, tk, q3, k3, vt)


def _merge_kernel(x_ref, yssd_ref, yret_ref, ymt_ref, ygla_ref, gpre_ref, wg_ref, bg_ref, wb_ref, wo_ref,
                  gpost_ref, o_ref):
    x = x_ref[...]
    h = _rms(x, gpre_ref[...]).astype(BF16)
    branches = (_dot(yssd_ref[...], wb_ref[0]), _dot(yret_ref[...], wb_ref[1]),
                _dot_tn(ymt_ref[...].reshape(BRANCH_WIDTH, -1), wb_ref[2]), _dot(ygla_ref[...], wb_ref[3]))
    merged = None
    for i, br in enumerate(branches):
        gate = jax.nn.sigmoid(_dot(h, wg_ref[:, i * D_MODEL:(i + 1) * D_MODEL]) + bg_ref[:, i * D_MODEL:(i + 1) * D_MODEL])
        merged = gate * br if merged is None else merged + gate * br
    o = _dot(merged.astype(BF16), wo_ref[...])
    o_ref[...] = x + _rms(o, gpost_ref[...])


def _merge(x3, y_ssd, y_ret, y_mla_t, y_gla, g_pre, wg, b_gate, wb, wo, g_post):
    b, l, _ = x3.shape
    tm = min(TOKEN_TILE, l)
    assert tm == y_mla_t.shape[-1]
    nt = l // tm
    w = BRANCH_WIDTH
    tok = lambda width: pl.BlockSpec((None, tm, width), lambda bi, j: (bi, j, 0))
    return pl.pallas_call(
        _merge_kernel,
        out_shape=jax.ShapeDtypeStruct((b, l, D_MODEL), F32),
        grid=(b, nt),
        in_specs=[tok(D_MODEL), tok(w), tok(w),
                  pl.BlockSpec((None, MLA_HEADS, None, MLA_V, tm), lambda bi, j: (bi, 0, j, 0, 0)), tok(w),
                  _const_spec((1, D_MODEL)), _const_spec(wg.shape), _const_spec((1, N_BRANCHES * D_MODEL)),
                  _const_spec(wb.shape), _const_spec(wo.shape), _const_spec((1, D_MODEL))],
        out_specs=tok(D_MODEL),
        compiler_params=_params("arbitrary", "arbitrary"),
        name="merge_out",
    )(x3, y_ssd.reshape(b, l, w), y_ret.reshape(b, l, w), y_mla_t, y_gla.reshape(b, l, w),
      g_pre, wg, b_gate.reshape(1, -1), wb, wo, g_post)


def _mlp_kernel(x_ref, gpre_ref, wi_ref, wo_ref, gpost_ref, o_ref):
    x = x_ref[...]
    h = _rms(x, gpre_ref[...]).astype(BF16)
    acc = None
    for j in range(D_FF // D_MODEL):
        u = jnp.maximum(_dot(h, wi_ref[:, j * D_MODEL:(j + 1) * D_MODEL]), 0.0)
        part = _dot((u * u).astype(BF16), wo_ref[j * D_MODEL:(j + 1) * D_MODEL, :])
        acc = part if acc is None else acc + part
    o_ref[...] = x + _rms(acc, gpost_ref[...])


def _mlp(x2, g_pre, wi, wo, g_post):
    t = x2.shape[0]
    tm = min(TOKEN_TILE, t)
    return pl.pallas_call(
        _mlp_kernel,
        out_shape=jax.ShapeDtypeStruct((t, D_MODEL), F32),
        grid=(t // tm,),
        in_specs=[pl.BlockSpec((tm, D_MODEL), lambda i: (i, 0)), _const_spec((1, D_MODEL)),
                  _const_spec(wi.shape), _const_spec(wo.shape), _const_spec((1, D_MODEL))],
        out_specs=pl.BlockSpec((tm, D_MODEL), lambda i: (i, 0)),
        compiler_params=_params("arbitrary"),
        name="mlp",
    )(x2, g_pre, wi, wo, g_post)


def _rot_perm(n_heads, dim):
    half = dim // 2
    return np.concatenate([h * dim + (np.arange(dim) + half) % dim for h in range(n_heads)])


def _layer_weights(w_in, mla_w_uq, mla_w_ukv):
    col = lambda a, b: w_in[:, a:b]
    w_ssd = col(_O_Z, _O_DT).astype(BF16)
    rq, rk = col(_O_RQ, _O_RK), col(_O_RK, _O_RV)
    perm = _rot_perm(RET_HEADS, RET_QK_DIM)
    w_ret = jnp.concatenate([rq, rk, rq[:, perm], rk[:, perm], col(_O_RV, _O_CQ)], axis=1).astype(BF16)
    kr = col(_O_KR, _O_GQ)
    zl = jnp.zeros((D_MODEL, MLA_PE_LANE), F32)
    zr = jnp.zeros((D_MODEL, LANES - MLA_PE_LANE - MLA_ROPE), F32)
    kperm = _rot_perm(1, MLA_ROPE)
    w_mla = jnp.concatenate([col(_O_CQ, _O_KR), zl, kr, zr, zl, kr[:, kperm], zr], axis=1).astype(BF16)
    w_gla = jnp.concatenate([col(_O_GQ, _O_GGK), col(_O_GG, _O_GATE)], axis=1).astype(BF16)
    w_small = jnp.concatenate(
        [col(_O_DT, _O_RQ), col(_O_GGK, _O_GG),
         jnp.zeros((D_MODEL, LANES - SSD_HEADS - GLA_GATE_RANK), F32)], axis=1).astype(BF16)
    w_gate = col(_O_GATE, _O_END).astype(BF16)

    hd = MLA_NOPE + MLA_ROPE
    uq = mla_w_uq.reshape(MLA_Q_LORA, MLA_HEADS, hd)
    padq = jnp.zeros((MLA_Q_LORA, MLA_HEADS, MLA_HEAD_PAD - hd), F32)
    wq = jnp.concatenate([uq, padq], axis=2).reshape(MLA_Q_LORA, -1).astype(BF16)
    uq_rot = uq[:, :, MLA_NOPE + kperm]
    wqr = jnp.concatenate([jnp.zeros((MLA_Q_LORA, MLA_HEADS, MLA_NOPE), F32), uq_rot, padq],
                          axis=2).reshape(MLA_Q_LORA, -1).astype(BF16)
    ukv = mla_w_ukv.reshape(MLA_KV_LORA, MLA_HEADS, MLA_NOPE + MLA_V)
    wk = jnp.concatenate([ukv[:, :, :MLA_NOPE],
                          jnp.zeros((MLA_KV_LORA, MLA_HEADS, MLA_HEAD_PAD - MLA_NOPE), F32)],
                         axis=2).reshape(MLA_KV_LORA, -1).astype(BF16)
    wvt = ukv[:, :, MLA_NOPE:].reshape(MLA_KV_LORA, -1).T.astype(BF16)
    return w_ssd, w_ret, w_mla, w_gla, w_small, w_gate, wq, wqr, wk, wvt


def _layer(x3, tabs, w_in, b_gate, ssd_conv_w, ssd_conv_b, ssd_dt_bias, ssd_a_log, ssd_d, ssd_norm, ret_norm,
           mla_q_norm, mla_w_uq, mla_kv_norm, mla_w_ukv, gla_w_gk2, gla_b_gk, gla_norm, w_branch, w_out,
           norm_pre_mix, norm_post_mix, norm_pre_mlp, norm_post_mlp, w_mlp_in, w_mlp_out):
    b, l, d = x3.shape
    c_ret, s_ret, c_mla, s_mla = tabs
    w_ssd, w_ret, w_mla, w_gla, w_small, w_gate, wq, wqr, wk, wvt = _layer_weights(w_in, mla_w_uq, mla_w_ukv)
    g_pre = norm_pre_mix.reshape(1, -1)
    x2 = x3.reshape(b * l, d)
    o_ssd, o_ret, o_mla, o_gla, o_small = _inproj(x2, g_pre, w_ssd, w_ret, w_mla, w_gla, w_small)
    y_ssd = _ssd(o_ssd, o_small, ssd_conv_w, ssd_conv_b, ssd_dt_bias, ssd_a_log, ssd_d, ssd_norm, b, l)
    y_ret = _ret(o_ret, c_ret, s_ret, ret_norm, b, l)
    q, k, vt = _mla_proj(o_mla, c_mla, s_mla, mla_q_norm, mla_kv_norm, wq, wqr, wk, wvt, b, l)
    y_mla_t = _flash(q, k, vt, b, l)
    y_gla = _gla(o_gla, o_small, gla_w_gk2, gla_b_gk, gla_norm, b, l)
    x3 = _merge(x3, y_ssd, y_ret, y_mla_t, y_gla, g_pre, w_gate, b_gate, w_branch.astype(BF16),
                w_out.astype(BF16), norm_post_mix.reshape(1, -1))
    x2 = _mlp(x3.reshape(b * l, d), norm_pre_mlp.reshape(1, -1), w_mlp_in.astype(BF16),
              w_mlp_out.astype(BF16), norm_post_mlp.reshape(1, -1))
    return x2.reshape(b, l, d)


def kernel(x, positions, w_in, b_gate, ssd_conv_w, ssd_conv_b, ssd_dt_bias, ssd_a_log, ssd_d, ssd_norm, ret_norm, mla_q_norm, mla_w_uq, mla_kv_norm, mla_w_ukv, gla_w_gk2, gla_b_gk, gla_norm, w_branch, w_out, norm_pre_mix, norm_post_mix, norm_pre_mlp, norm_post_mlp, w_mlp_in, w_mlp_out):
    per_layer = (w_in, b_gate, ssd_conv_w, ssd_conv_b, ssd_dt_bias, ssd_a_log, ssd_d, ssd_norm, ret_norm,
                 mla_q_norm, mla_w_uq, mla_kv_norm, mla_w_ukv, gla_w_gk2, gla_b_gk, gla_norm, w_branch, w_out,
                 norm_pre_mix, norm_post_mix, norm_pre_mlp, norm_post_mlp, w_mlp_in, w_mlp_out)
    tabs = _rope_tables(positions)
    for i in range(w_in.shape[0]):
        x = _layer(x, tabs, *(p[i] for p in per_layer))
    return x
```

```python
import functools

import numpy as np
import jax
import jax.numpy as jnp
from jax import lax
from jax.experimental import pallas as pl
from jax.experimental.pallas import tpu as pltpu

F32 = jnp.float32
BF16 = jnp.bfloat16

D_MODEL = 1024
SSD_HEADS = 8
SSD_HEAD_DIM = 64
SSD_INNER = 512
SSD_GROUPS = 2
SSD_STATE = 128
SSD_CONV = 4
SSD_CHUNK = 128
SSD_XBC = 1024
RET_HEADS = 4
RET_QK_DIM = 64
RET_V_DIM = 128
RET_CHUNK = 128
MLA_HEADS = 8
MLA_Q_LORA = 256
MLA_KV_LORA = 128
MLA_NOPE = 64
MLA_ROPE = 32
MLA_V = 64
GLA_HEADS = 4
GLA_K_DIM = 64
GLA_V_DIM = 128
GLA_GATE_RANK = 16
GLA_GATE_NORM = 16.0
GLA_CHUNK = 64
N_BRANCHES = 4
BRANCH_WIDTH = 512
D_FF = 4 * D_MODEL
ROPE_THETA = 10000.0
RMS_EPS = 1e-6

LANES = 128
MLA_HEAD_PAD = 128
VT_ROWS = MLA_V + 16
LOG2_E = 1.4426950408889634
VMEM_LIMIT = 56 * 1024 * 1024

TOKEN_TILE = 512
ATTN_BLOCK_Q = 512
ATTN_BLOCK_K = 512
ATTN_UNROLL = 8
GLA_STEP = 256
RET_STEP = 256
SSD_STEP = 256

_SIZES = (SSD_INNER, SSD_XBC, SSD_HEADS,
          RET_HEADS * RET_QK_DIM, RET_HEADS * RET_QK_DIM, RET_HEADS * RET_V_DIM, RET_HEADS * RET_V_DIM,
          MLA_Q_LORA, MLA_KV_LORA, MLA_ROPE,
          GLA_HEADS * GLA_K_DIM, GLA_HEADS * GLA_K_DIM, GLA_HEADS * GLA_V_DIM, GLA_GATE_RANK,
          GLA_HEADS * GLA_V_DIM, N_BRANCHES * D_MODEL)
_OFFS = tuple(int(v) for v in np.concatenate([[0], np.cumsum(_SIZES)]))
(_O_Z, _O_XBC, _O_DT, _O_RQ, _O_RK, _O_RV, _O_RG, _O_CQ, _O_CKV, _O_KR,
 _O_GQ, _O_GK, _O_GV, _O_GGK, _O_GG, _O_GATE, _O_END) = _OFFS

SMALL_DT_LANE = 0
SMALL_GGK_LANE = 8
MLA_PE_LANE = MLA_NOPE


def _dot(a, b, precision=None):
    return jnp.dot(a, b, preferred_element_type=F32, precision=precision)


def _dot_nt(a, b):
    return lax.dot_general(a, b, (((1,), (1,)), ((), ())), preferred_element_type=F32)


def _dot_tn(a, b):
    return lax.dot_general(a, b, (((0,), (0,)), ((), ())), preferred_element_type=F32)


def _rms(x, g):
    return x * lax.rsqrt(jnp.mean(x * x, axis=-1, keepdims=True) + RMS_EPS) * g


def _silu(x):
    return x * jax.nn.sigmoid(x)


def _softplus(x):
    return jnp.maximum(x, 0.0) + jnp.log1p(jnp.exp(-jnp.abs(x)))


def _params(*sem):
    return pltpu.CompilerParams(dimension_semantics=sem, vmem_limit_bytes=VMEM_LIMIT)


def _const_spec(shape):
    nd = len(shape)
    return pl.BlockSpec(shape, lambda *_: (0,) * nd, pipeline_mode=pl.Buffered(1))


def _tril(n):
    r = lax.broadcasted_iota(jnp.int32, (n, n), 0)
    c = lax.broadcasted_iota(jnp.int32, (n, n), 1)
    return r >= c


def _rope_kernel(pos_ref, c_ref, cr_ref, sr_ref, cm_ref, sm_ref):
    pos = pos_ref[...]
    c = c_ref[...]
    ang_r = pos * c[0:1, :]
    cr_ref[...] = jnp.cos(ang_r)
    sr_ref[...] = jnp.sin(ang_r) * c[1:2, :]
    ang_m = pos * c[2:3, :]
    cm_ref[...] = jnp.cos(ang_m)
    sm_ref[...] = jnp.sin(ang_m) * c[3:4, :]


def _rope_consts():
    lane = np.arange(LANES)
    half_r = RET_QK_DIM // 2
    half_m = MLA_ROPE // 2
    inv_r = ROPE_THETA ** (-jnp.arange(half_r, dtype=F32) / half_r)
    inv_m = ROPE_THETA ** (-jnp.arange(half_m, dtype=F32) / half_m)
    freq_r = inv_r[lane % half_r]
    sign_r = jnp.asarray(np.where((lane % RET_QK_DIM) < half_r, -1.0, 1.0), F32)
    in_pe = (lane >= MLA_PE_LANE) & (lane < MLA_PE_LANE + MLA_ROPE)
    freq_m = jnp.where(in_pe, inv_m[(lane - MLA_PE_LANE) % half_m], 0.0)
    sign_m = jnp.asarray(np.where(in_pe, np.where((lane - MLA_PE_LANE) < half_m, -1.0, 1.0), 0.0), F32)
    z = jnp.zeros((LANES,), F32)
    return jnp.stack([freq_r, sign_r, freq_m, sign_m, z, z, z, z]).astype(F32)


def _rope_tables(positions):
    b, l = positions.shape
    t = b * l
    tb = min(1024, t)
    pos = positions.astype(F32).reshape(t, 1)
    tab = jax.ShapeDtypeStruct((t, LANES), F32)
    spec = pl.BlockSpec((tb, LANES), lambda i: (i, 0))
    return pl.pallas_call(
        _rope_kernel,
        out_shape=(tab, tab, tab, tab),
        grid=(t // tb,),
        in_specs=[pl.BlockSpec((tb, 1), lambda i: (i, 0)), _const_spec((8, LANES))],
        out_specs=(spec, spec, spec, spec),
        compiler_params=_params("arbitrary"),
        name="rope_tables",
    )(pos, _rope_consts())


def _inproj_kernel(x_ref, g_ref, wssd_ref, wret_ref, wmla_ref, wgla_ref, wsm_ref,
                   ossd_ref, oret_ref, omla_ref, ogla_ref, osm_ref):
    h = _rms(x_ref[...], g_ref[...]).astype(BF16)
    ossd_ref[...] = _dot(h, wssd_ref[...]).astype(BF16)
    oret_ref[...] = _dot(h, wret_ref[...]).astype(BF16)
    omla_ref[...] = _dot(h, wmla_ref[...]).astype(BF16)
    ogla_ref[...] = _dot(h, wgla_ref[...]).astype(BF16)
    osm_ref[...] = _dot(h, wsm_ref[...])


def _inproj(x2, g, wssd, wret, wmla, wgla, wsm):
    t = x2.shape[0]
    tm = min(TOKEN_TILE, t)
    ws = (wssd, wret, wmla, wgla, wsm)
    outs = tuple(jax.ShapeDtypeStruct((t, w.shape[1]), BF16) for w in ws[:4]) + (
        jax.ShapeDtypeStruct((t, LANES), F32),)
    return pl.pallas_call(
        _inproj_kernel,
        out_shape=outs,
        grid=(t // tm,),
        in_specs=[pl.BlockSpec((tm, D_MODEL), lambda i: (i, 0)), _const_spec((1, D_MODEL))]
                 + [_const_spec(w.shape) for w in ws],
        out_specs=tuple(pl.BlockSpec((tm, o.shape[1]), lambda i: (i, 0)) for o in outs),
        compiler_params=_params("arbitrary"),
        name="in_proj",
    )(x2, g, *ws)


def _split3(x):
    hi = x.astype(BF16)
    r = x - hi.astype(F32)
    mid = r.astype(BF16)
    lo = (r - mid.astype(F32)).astype(BF16)
    return hi, mid, lo


def _dot_exact01(a01, x):
    n = x.shape[1]
    parts = _dot(a01, jnp.concatenate(_split3(x), axis=1))
    return parts[:, 0:n] + parts[:, n:2 * n] + parts[:, 2 * n:3 * n]


def _chunk_tril(rows, chunk):
    r = lax.broadcasted_iota(jnp.int32, (rows, rows), 0)
    c = lax.broadcasted_iota(jnp.int32, (rows, rows), 1)
    shift = chunk.bit_length() - 1
    return ((r >> shift) == (c >> shift)) & (r >= c)


def _ssd_kernel(blk_ref, sm_ref, cw_ref, cb_ref, shift_ref, dtb_ref, alog_ref, exp_ref, dsk_ref, ng_ref,
                y_ref, conv_ref, state_ref):
    q = SSD_CHUNK
    rows = blk_ref.shape[0]
    n_chunks = rows // q
    tail = 8

    @pl.when(pl.program_id(1) == 0)
    def _():
        conv_ref[0:tail, :] = jnp.zeros((tail, SSD_XBC), F32)
        state_ref[...] = jnp.zeros_like(state_ref)

    z = blk_ref[:, 0:SSD_INNER].astype(F32)
    xbc_b = blk_ref[:, SSD_INNER:SSD_INNER + SSD_XBC]
    xbc = xbc_b.astype(F32)
    conv_ref[tail:2 * tail, :] = xbc[0:tail, :]
    acc = cb_ref[...] + cw_ref[SSD_CONV - 1:SSD_CONV, :] * xbc
    head = acc[0:tail, :]
    for j in range(SSD_CONV - 1):
        off = tail - (SSD_CONV - 1) + j
        acc = acc + cw_ref[j:j + 1, :] * _dot(shift_ref[j], xbc_b)
        head = head + cw_ref[j:j + 1, :] * conv_ref[off:off + tail, :]
    conv_ref[0:tail, :] = xbc[rows - tail:rows, :]
    xbc = _silu(jnp.concatenate([head, acc[tail:, :]], axis=0))
    xs = xbc[:, 0:SSD_INNER]
    gn = SSD_GROUPS * SSD_STATE
    bm = xbc[:, SSD_INNER:SSD_INNER + gn].astype(BF16)
    cm = xbc[:, SSD_INNER + gn:SSD_INNER + 2 * gn].astype(BF16)

    dt = _softplus(sm_ref[...] + dtb_ref[...])
    la = dt * (-jnp.exp(alog_ref[...]))
    tri = jnp.where(_chunk_tril(rows, q), 1.0, 0.0).astype(BF16)
    cum = _dot_exact01(tri, la)
    expand3 = exp_ref[...]
    cum_ch = _dot(jnp.concatenate(_split3(cum), axis=1), expand3)
    dt_ch = _dot(jnp.concatenate(_split3(dt), axis=1), expand3)
    xdt = xs * dt_ch
    xdt_b = xdt.astype(BF16)
    last = jnp.concatenate(
        [jnp.broadcast_to(cum_ch[(j + 1) * q - 1:(j + 1) * q, :], (q, SSD_INNER)) for j in range(n_chunks)], axis=0)
    xdte_b = (xdt * jnp.exp(last - cum_ch)).astype(BF16)

    hpg = SSD_HEADS // SSD_GROUPS
    gw = hpg * SSD_HEAD_DIM
    lane_lo = lax.broadcasted_iota(jnp.int32, (q, LANES), 1) < SSD_HEAD_DIM
    mask = _tril(q)
    state = [state_ref[g] for g in range(SSD_GROUPS)]
    y_rows = []
    for c in range(n_chunks):
        sl = slice(c * q, (c + 1) * q)
        cum_c = cum[sl]
        cum_t = cum_c.T
        chunk_decay = jnp.exp(cum_ch[(c + 1) * q - 1:(c + 1) * q, :])
        ydiag, yoff = [], []
        for g in range(SSD_GROUPS):
            cg = cm[sl, g * SSD_STATE:(g + 1) * SSD_STATE]
            bg = bm[sl, g * SSD_STATE:(g + 1) * SSD_STATE]
            s = _dot_nt(cg, bg)
            yoff.append(_dot(cg, state[g].astype(BF16)))
            for j in range(hpg // 2):
                xpair = xdt_b[sl, g * gw + j * LANES:g * gw + (j + 1) * LANES]
                ys = []
                for e in range(2):
                    h = g * hpg + 2 * j + e
                    seg = cum_c[:, h:h + 1] - cum_t[h:h + 1, :]
                    a = (s * jnp.exp(jnp.where(mask, seg, -jnp.inf))).astype(BF16)
                    ys.append(_dot(a, xpair))
                ydiag.append(jnp.where(lane_lo, ys[0], ys[1]))
            state[g] = (state[g] * chunk_decay[:, g * gw:(g + 1) * gw]
                        + _dot_tn(bg, xdte_b[sl, g * gw:(g + 1) * gw]))
        y_rows.append(jnp.concatenate(ydiag, axis=1) + jnp.concatenate(yoff, axis=1) * jnp.exp(cum_ch[sl]))
    for g in range(SSD_GROUPS):
        state_ref[g] = state[g]
    y = jnp.concatenate(y_rows, axis=0)
    y = (y + dsk_ref[...] * xs) * _silu(z)
    outs = []
    for g in range(SSD_GROUPS):
        outs.append(_rms(y[:, g * gw:(g + 1) * gw], ng_ref[:, g * gw:(g + 1) * gw]))
    y_ref[...] = jnp.concatenate(outs, axis=1).astype(BF16)


def _ssd(o_ssd, o_small, conv_w, conv_b, dt_bias, a_log, d_skip, norm_g, b, l):
    rows = min(SSD_STEP, l)
    nc = l // rows
    pad = LANES - SSD_HEADS
    dtb = jnp.pad(dt_bias, (0, pad)).reshape(1, LANES)
    alog = jnp.pad(a_log, (0, pad)).reshape(1, LANES)
    head_of = np.arange(SSD_INNER) // SSD_HEAD_DIM
    expand = np.arange(LANES)[:, None] == head_of[None, :]
    expand3 = jnp.asarray(np.concatenate([expand] * 3, axis=0), BF16)
    dsk = jnp.repeat(d_skip, SSD_HEAD_DIM).reshape(1, SSD_INNER)
    t = np.arange(rows)
    shifts = jnp.asarray(np.stack([t[None, :] == t[:, None] - (SSD_CONV - 1 - j) for j in range(SSD_CONV - 1)]), BF16)
    row = lambda bi, ci: (bi * nc + ci, 0)
    return pl.pallas_call(
        _ssd_kernel,
        out_shape=jax.ShapeDtypeStruct((b * l, SSD_INNER), BF16),
        grid=(b, nc),
        in_specs=[pl.BlockSpec((rows, SSD_INNER + SSD_XBC), row), pl.BlockSpec((rows, LANES), row),
                  _const_spec((SSD_CONV, SSD_XBC)), _const_spec((1, SSD_XBC)), _const_spec(shifts.shape),
                  _const_spec((1, LANES)), _const_spec((1, LANES)), _const_spec(expand3.shape),
                  _const_spec((1, SSD_INNER)), _const_spec((1, SSD_INNER))],
        out_specs=pl.BlockSpec((rows, SSD_INNER), row),
        scratch_shapes=[pltpu.VMEM((16, SSD_XBC), F32),
                        pltpu.VMEM((SSD_GROUPS, SSD_STATE, SSD_INNER // SSD_GROUPS), F32)],
        compiler_params=_params("arbitrary", "arbitrary"),
        name="ssd_mixer",
    )(o_ssd, o_small, conv_w, conv_b.reshape(1, -1), shifts, dtb, alog, expand3, dsk, norm_g.reshape(1, -1))


def _head_rows(full, n_heads, rows, cols):
    return jnp.concatenate([full[h * rows:(h + 1) * rows, h * cols:(h + 1) * cols] for h in range(n_heads)], axis=0)


def _ret_kernel(blk_ref, c_ref, s_ref, dm_ref, qs_ref, ks_ref, cd_ref, ng_ref, y_ref, state_ref):
    @pl.when(pl.program_id(1) == 0)
    def _():
        state_ref[...] = jnp.zeros_like(state_ref)

    w = RET_HEADS * RET_QK_DIM
    vw = RET_HEADS * RET_V_DIM
    ct = jnp.concatenate([c_ref[...]] * (w // LANES), axis=1)
    st = jnp.concatenate([s_ref[...]] * (w // LANES), axis=1)
    q = blk_ref[:, 0:w].astype(F32) * ct + blk_ref[:, 2 * w:3 * w].astype(F32) * st
    k = (blk_ref[:, w:2 * w].astype(F32) * ct + blk_ref[:, 3 * w:4 * w].astype(F32) * st) * (RET_QK_DIM ** -0.5)
    v = blk_ref[:, 4 * w:4 * w + vw]
    g = blk_ref[:, 4 * w + vw:4 * w + 2 * vw].astype(F32)
    k_b = k.astype(BF16)
    qo = q * qs_ref[...]
    lane = lax.broadcasted_iota(jnp.int32, (1, w), 1)
    state = state_ref[...]
    state_b = state.astype(BF16)
    ys = []
    for h in range(RET_HEADS):
        in_head = (lane >= h * RET_QK_DIM) & (lane < (h + 1) * RET_QK_DIM)
        qh = jnp.where(in_head, q, 0.0).astype(BF16)
        sc = (_dot_nt(qh, k_b) * dm_ref[h]).astype(BF16)
        yh = _dot(sc, v[:, h * RET_V_DIM:(h + 1) * RET_V_DIM]) + _dot(jnp.where(in_head, qo, 0.0).astype(BF16), state_b)
        ys.append(_rms(yh, ng_ref[:, h * RET_V_DIM:(h + 1) * RET_V_DIM]))
    upd = _head_rows(_dot_tn((k * ks_ref[...]).astype(BF16), v), RET_HEADS, RET_QK_DIM, RET_V_DIM)
    state_ref[...] = state * cd_ref[...] + upd
    y_ref[...] = (jnp.concatenate(ys, axis=1) * _silu(g)).astype(BF16)


def _ret_consts(rows):
    lg = np.log1p(-np.exp2(-5.0 - np.arange(RET_HEADS, dtype=np.float64)))
    i = np.arange(rows)
    dm = np.where(i[:, None] >= i[None, :], np.exp(lg[:, None, None] * (i[:, None] - i[None, :])[None]), 0.0)
    qs = np.repeat(np.exp(lg[None, :] * (i[:, None] + 1)), RET_QK_DIM, axis=1)
    ks = np.repeat(np.exp(lg[None, :] * (rows - 1 - i[:, None])), RET_QK_DIM, axis=1)
    cd = np.broadcast_to(np.repeat(np.exp(lg * rows), RET_QK_DIM)[:, None], (RET_HEADS * RET_QK_DIM, RET_V_DIM))
    return tuple(jnp.asarray(a, F32) for a in (dm, qs, ks, cd))


def _ret(o_ret, c_tab, s_tab, norm_g, b, l):
    rows = min(RET_STEP, l)
    nc = l // rows
    w = RET_HEADS * RET_QK_DIM
    vw = RET_HEADS * RET_V_DIM
    dm, qs, ks, cd = _ret_consts(rows)
    row = lambda bi, ci: (bi * nc + ci, 0)
    return pl.pallas_call(
        _ret_kernel,
        out_shape=jax.ShapeDtypeStruct((b * l, vw), BF16),
        grid=(b, nc),
        in_specs=[pl.BlockSpec((rows, 4 * w + 2 * vw), row), pl.BlockSpec((rows, LANES), row),
                  pl.BlockSpec((rows, LANES), row),
                  _const_spec(dm.shape), _const_spec(qs.shape), _const_spec(ks.shape),
                  _const_spec(cd.shape), _const_spec((1, vw))],
        out_specs=pl.BlockSpec((rows, vw), row),
        scratch_shapes=[pltpu.VMEM((w, RET_V_DIM), F32)],
        compiler_params=_params("arbitrary", "arbitrary"),
        name="ret_mixer",
    )(o_ret, c_tab, s_tab, dm, qs, ks, cd, norm_g.reshape(1, -1))


def _gla_kernel(blk_ref, sm_ref, w2_ref, bgk_ref, ng_ref, y_ref, state_ref):
    @pl.when(pl.program_id(1) == 0)
    def _():
        state_ref[...] = jnp.zeros_like(state_ref)

    c = GLA_CHUNK
    rows = blk_ref.shape[0]
    kw = GLA_HEADS * GLA_K_DIM
    vw = GLA_HEADS * GLA_V_DIM
    sm_hi, sm_mid, _ = _split3(sm_ref[...])
    logits = _dot(jnp.concatenate([sm_hi, sm_hi, sm_mid], axis=1), w2_ref[...]) + bgk_ref[...]
    log_g = -_softplus(-logits) * (1.0 / GLA_GATE_NORM)
    tri = _chunk_tril(rows, c)
    cum = _dot_exact01(jnp.where(tri, 1.0, 0.0).astype(BF16), log_g)
    n_chunks = rows // c
    total = jnp.concatenate(
        [jnp.broadcast_to(cum[(j + 1) * c - 1:(j + 1) * c, :], (c, kw)) for j in range(n_chunks)], axis=0)
    q = blk_ref[:, 0:kw].astype(F32) * (GLA_K_DIM ** -0.5)
    k = blk_ref[:, kw:2 * kw].astype(F32)
    v = blk_ref[:, 2 * kw:2 * kw + vw]
    g = blk_ref[:, 2 * kw + vw:2 * kw + 2 * vw].astype(F32)
    q_in = q * jnp.exp(cum)
    k_in = (k * jnp.exp(-cum)).astype(BF16)
    k_st = (k * jnp.exp(total - cum)).astype(BF16)
    lane = lax.broadcasted_iota(jnp.int32, (1, kw), 1)
    in_head = [(lane >= h * GLA_K_DIM) & (lane < (h + 1) * GLA_K_DIM) for h in range(GLA_HEADS)]

    state = state_ref[...]
    o_inter = []
    for j in range(n_chunks):
        sl = slice(j * c, (j + 1) * c)
        q_stack = jnp.concatenate([jnp.where(m, q_in[sl], 0.0) for m in in_head], axis=0).astype(BF16)
        k_stack = jnp.concatenate([jnp.where(m, k_st[sl], 0) for m in in_head], axis=0)
        v_stack = jnp.concatenate([v[sl, h * GLA_V_DIM:(h + 1) * GLA_V_DIM] for h in range(GLA_HEADS)], axis=0)
        oi = _dot(q_stack, state.astype(BF16))
        o_inter.append(jnp.concatenate([oi[h * c:(h + 1) * c] for h in range(GLA_HEADS)], axis=1))
        last = cum[(j + 1) * c - 1:(j + 1) * c, :]
        decay = jnp.exp(jnp.broadcast_to(last, (LANES, kw)).T)
        state = state * decay + _dot_tn(k_stack, v_stack)
    state_ref[...] = state
    o_inter = jnp.concatenate(o_inter, axis=0)
    outs = []
    for h in range(GLA_HEADS):
        qh = jnp.where(in_head[h], q_in, 0.0).astype(BF16)
        sc = jnp.where(tri, _dot_nt(qh, k_in), 0.0).astype(BF16)
        oh = _dot(sc, v[:, h * GLA_V_DIM:(h + 1) * GLA_V_DIM]) + o_inter[:, h * GLA_V_DIM:(h + 1) * GLA_V_DIM]
        outs.append(_rms(oh, ng_ref[:, h * GLA_V_DIM:(h + 1) * GLA_V_DIM]))
    y_ref[...] = (jnp.concatenate(outs, axis=1) * _silu(g)).astype(BF16)


def _gla(o_gla, o_small, w_gk2, b_gk, norm_g, b, l):
    rows = min(GLA_STEP, l)
    ns = l // rows
    kw = GLA_HEADS * GLA_K_DIM
    vw = GLA_HEADS * GLA_V_DIM
    w2 = jnp.zeros((LANES, kw), F32).at[SMALL_GGK_LANE:SMALL_GGK_LANE + GLA_GATE_RANK].set(w_gk2)
    w2_hi = w2.astype(BF16)
    w2_mid = (w2 - w2_hi.astype(F32)).astype(BF16)
    w2 = jnp.concatenate([w2_hi, w2_mid, w2_hi], axis=0)
    row = lambda bi, ci: (bi * ns + ci, 0)
    return pl.pallas_call(
        _gla_kernel,
        out_shape=jax.ShapeDtypeStruct((b * l, vw), BF16),
        grid=(b, ns),
        in_specs=[pl.BlockSpec((rows, 2 * kw + 2 * vw), row), pl.BlockSpec((rows, LANES), row),
                  _const_spec(w2.shape), _const_spec((1, kw)), _const_spec((1, vw))],
        out_specs=pl.BlockSpec((rows, vw), row),
        scratch_shapes=[pltpu.VMEM((kw, GLA_V_DIM), F32)],
        compiler_params=_params("arbitrary", "arbitrary"),
        name="gla_mixer",
    )(o_gla, o_small, w2, b_gk.reshape(1, -1), norm_g.reshape(1, -1))


def _mla_proj_kernel(blk_ref, c_ref, s_ref, qn_ref, kvn_ref, wq_ref, wqr_ref, wk_ref, wvt_ref,
                     q_ref, k_ref, vt_ref):
    o_ckv = MLA_Q_LORA
    o_kr = o_ckv + MLA_KV_LORA
    cq = _rms(blk_ref[:, 0:MLA_Q_LORA].astype(F32), qn_ref[...]).astype(BF16)
    ckv = _rms(blk_ref[:, o_ckv:o_kr].astype(F32), kvn_ref[...]).astype(BF16)
    c = c_ref[...]
    s = s_ref[...]
    ct = jnp.concatenate([c] * MLA_HEADS, axis=1)
    st = jnp.concatenate([s] * MLA_HEADS, axis=1)
    scale = (MLA_NOPE + MLA_ROPE) ** -0.5 * LOG2_E
    q = (_dot(cq, wq_ref[...]) * ct + _dot(cq, wqr_ref[...]) * st) * scale
    q_ref[...] = q.astype(BF16)
    kpe = blk_ref[:, o_kr:o_kr + LANES].astype(F32) * c + blk_ref[:, o_kr + LANES:o_kr + 2 * LANES].astype(F32) * s
    k = _dot(ckv, wk_ref[...]) + jnp.concatenate([kpe] * MLA_HEADS, axis=1)
    k_ref[...] = k.astype(BF16)
    vt = _dot_nt(wvt_ref[...], ckv).astype(BF16)
    ones = jnp.ones((VT_ROWS - MLA_V, vt.shape[1]), BF16)
    for h in range(MLA_HEADS):
        vt_ref[h, 0:MLA_V, :] = vt[h * MLA_V:(h + 1) * MLA_V, :]
        vt_ref[h, MLA_V:VT_ROWS, :] = ones


def _mla_proj(o_mla, c_tab, s_tab, q_norm, kv_norm, wq, wqr, wk, wvt, b, l):
    tm = min(ATTN_BLOCK_K, l)
    nk = l // tm
    hw = MLA_HEADS * MLA_HEAD_PAD
    row = lambda bi, j: (bi * nk + j, 0)
    return pl.pallas_call(
        _mla_proj_kernel,
        out_shape=(jax.ShapeDtypeStruct((b * l, hw), BF16), jax.ShapeDtypeStruct((b * l, hw), BF16),
                   jax.ShapeDtypeStruct((b, nk, MLA_HEADS, VT_ROWS, tm), BF16)),
        grid=(b, nk),
        in_specs=[pl.BlockSpec((tm, o_mla.shape[1]), row), pl.BlockSpec((tm, LANES), row),
                  pl.BlockSpec((tm, LANES), row),
                  _const_spec((1, MLA_Q_LORA)), _const_spec((1, MLA_KV_LORA)),
                  _const_spec(wq.shape), _const_spec(wqr.shape), _const_spec(wk.shape), _const_spec(wvt.shape)],
        out_specs=(pl.BlockSpec((tm, hw), row), pl.BlockSpec((tm, hw), row),
                   pl.BlockSpec((None, None, MLA_HEADS, VT_ROWS, tm), lambda bi, j: (bi, j, 0, 0, 0))),
        compiler_params=_params("arbitrary", "arbitrary"),
        name="mla_proj",
    )(o_mla, c_tab, s_tab, q_norm.reshape(1, -1), kv_norm.reshape(1, -1), wq, wqr, wk, wvt)


def _flash_kernel(nq, tq_ref, tk_ref, q_ref, k_ref, vt_ref, o_ref, m_ref, acc_ref, s_ref, mb0_ref, mb1_ref):
    bq, bk = ATTN_BLOCK_Q, ATTN_BLOCK_K
    half = bq // 2
    n_off = nq * (nq - 1) // 2
    n_blk = n_off + nq
    m_ref[...] = jnp.full(m_ref.shape, -jnp.inf, F32)
    acc_ref[...] = jnp.zeros(acc_ref.shape, F32)
    mb_refs = (mb0_ref, mb1_ref)

    def produce(t, slot, diagonal):
        q = q_ref[pl.ds(pl.multiple_of(tq_ref[t] * bq, bq), bq), :]
        k = k_ref[pl.ds(pl.multiple_of(tk_ref[t] * bk, bk), bk), :]
        s = _dot_nt(k, q)
        if diagonal:
            kpos = lax.broadcasted_iota(jnp.int32, (bk, bq), 0)
            qpos = lax.broadcasted_iota(jnp.int32, (bk, bq), 1)
            s = jnp.where(kpos <= qpos, s, -jnp.inf)
        s_ref[0:bk, slot * half:(slot + 1) * half] = s[:, 0:half]
        s_ref[bk:2 * bk, slot * half:(slot + 1) * half] = s[:, half:bq]
        mb_refs[slot][...] = jnp.max(s, axis=0, keepdims=True)

    def consume(t, slot, diagonal):
        qi = tq_ref[t]
        m_old = m_ref[qi]
        m_new = jnp.maximum(m_old, mb_refs[slot][...])
        alpha = jnp.exp2(m_old - m_new)
        s = jnp.concatenate([s_ref[0:bk, slot * half:(slot + 1) * half],
                             s_ref[bk:2 * bk, slot * half:(slot + 1) * half]], axis=1)
        p = jnp.exp2(s - m_new).astype(BF16)
        acc = acc_ref[qi] * alpha + _dot(vt_ref[tk_ref[t]], p)
        if diagonal:
            o_ref[qi] = (acc[0:MLA_V, :] / acc[MLA_V:MLA_V + 1, :]).astype(BF16)
        else:
            acc_ref[qi] = acc
            m_ref[qi] = m_new

    def static_steps(t_from, t_to):
        for t in range(t_from, t_to):
            if t + 1 < n_blk:
                produce(t + 1, (t + 1) % 2, t + 1 >= n_off)
            consume(t, t % 2, t >= n_off)

    u = ATTN_UNROLL

    def group_loop(t_from, n_groups, diagonal):
        def group(i, carry):
            t0 = t_from + u * i
            for d in range(u):
                produce(t0 + d + 1, (d + 1) % 2, diagonal)
                consume(t0 + d, d % 2, diagonal)
            return carry

        if n_groups > 0:
            assert t_from % 2 == 0 and u % 2 == 0
            lax.fori_loop(0, n_groups, group, 0)

    produce(0, 0, n_off == 0)
    n_a = max(0, (n_off - 1) // u)
    group_loop(0, n_a, False)
    t_b = min(n_off + n_off % 2, n_blk)
    static_steps(u * n_a, t_b)
    n_b = max(0, (n_blk - 1 - t_b) // u)
    group_loop(t_b, n_b, True)
    static_steps(t_b + u * n_b, n_blk)


def _flash_order(nq):
    off = [(qi, kb) for qi in range(nq) for kb in range(qi)]
    blocks = off + [(qi, qi) for qi in range(nq)]
    return (jnp.asarray([b[0] for b in blocks], jnp.int32), jnp.asarray([b[1] for b in blocks], jnp.int32))


def _flash(q, k, vt, b, l):
    bq, bk = ATTN_BLOCK_Q, ATTN_BLOCK_K
    assert bq == bk
    nq = l // bq
    nk = l // bk
    q3 = q.reshape(b, l, MLA_HEADS * MLA_HEAD_PAD)
    k3 = k.reshape(b, l, MLA_HEADS * MLA_HEAD_PAD)
    tq, tk = _flash_order(nq)
    grid_spec = pltpu.PrefetchScalarGridSpec(
        num_scalar_prefetch=2,
        grid=(b, MLA_HEADS),
        in_specs=[pl.BlockSpec((None, l, MLA_HEAD_PAD), lambda bi, h, *_: (bi, 0, h)),
                  pl.BlockSpec((None, l, MLA_HEAD_PAD), lambda bi, h, *_: (bi, 0, h)),
                  pl.BlockSpec((None, nk, None, VT_ROWS, bk), lambda bi, h, *_: (bi, 0, h, 0, 0))],
        out_specs=pl.BlockSpec((None, None, nq, MLA_V, bq), lambda bi, h, *_: (bi, h, 0, 0, 0)),
        scratch_shapes=[pltpu.VMEM((nq, 1, bq), F32), pltpu.VMEM((nq, VT_ROWS, bq), F32),
                        pltpu.VMEM((2 * bk, bq), F32),
                        pltpu.VMEM((1, bq), F32), pltpu.VMEM((1, bq), F32)])
    return pl.pallas_call(
        functools.partial(_flash_kernel, nq),
        out_shape=jax.ShapeDtypeStruct((b, MLA_HEADS, nq, MLA_V, bq), BF16),
        grid_spec=grid_spec,
        compiler_params=_params("arbitrary", "arbitrary"),
        name="mla_flash",
    )(tq, tk, q3, k3, vt)


def _merge_kernel(x_ref, yssd_ref, yret_ref, ymt_ref, ygla_ref, gpre_ref, wg_ref, bg_ref, wb_ref, wo_ref,
                  gpost_ref, o_ref):
    x = x_ref[...]
    h = _rms(x, gpre_ref[...]).astype(BF16)
    branches = (_dot(yssd_ref[...], wb_ref[0]), _dot(yret_ref[...], wb_ref[1]),
                _dot_tn(ymt_ref[...].reshape(BRANCH_WIDTH, -1), wb_ref[2]), _dot(ygla_ref[...], wb_ref[3]))
    merged = None
    for i, br in enumerate(branches):
        gate = jax.nn.sigmoid(_dot(h, wg_ref[:, i * D_MODEL:(i + 1) * D_MODEL]) + bg_ref[:, i * D_MODEL:(i + 1) * D_MODEL])
        merged = gate * br if merged is None else merged + gate * br
    o = _dot(merged.astype(BF16), wo_ref[...])
    o_ref[...] = x + _rms(o, gpost_ref[...])


def _merge(x3, y_ssd, y_ret, y_mla_t, y_gla, g_pre, wg, b_gate, wb, wo, g_post):
    b, l, _ = x3.shape
    tm = min(TOKEN_TILE, l)
    assert tm == y_mla_t.shape[-1]
    nt = l // tm
    w = BRANCH_WIDTH
    tok = lambda width: pl.BlockSpec((None, tm, width), lambda bi, j: (bi, j, 0))
    return pl.pallas_call(
        _merge_kernel,
        out_shape=jax.ShapeDtypeStruct((b, l, D_MODEL), F32),
        grid=(b, nt),
        in_specs=[tok(D_MODEL), tok(w), tok(w),
                  pl.BlockSpec((None, MLA_HEADS, None, MLA_V, tm), lambda bi, j: (bi, 0, j, 0, 0)), tok(w),
                  _const_spec((1, D_MODEL)), _const_spec(wg.shape), _const_spec((1, N_BRANCHES * D_MODEL)),
                  _const_spec(wb.shape), _const_spec(wo.shape), _const_spec((1, D_MODEL))],
        out_specs=tok(D_MODEL),
        compiler_params=_params("arbitrary", "arbitrary"),
        name="merge_out",
    )(x3, y_ssd.reshape(b, l, w), y_ret.reshape(b, l, w), y_mla_t, y_gla.reshape(b, l, w),
      g_pre, wg, b_gate.reshape(1, -1), wb, wo, g_post)


def _mlp_kernel(x_ref, gpre_ref, wi_ref, wo_ref, gpost_ref, o_ref):
    x = x_ref[...]
    h = _rms(x, gpre_ref[...]).astype(BF16)
    acc = None
    for j in range(D_FF // D_MODEL):
        u = jnp.maximum(_dot(h, wi_ref[:, j * D_MODEL:(j + 1) * D_MODEL]), 0.0)
        part = _dot((u * u).astype(BF16), wo_ref[j * D_MODEL:(j + 1) * D_MODEL, :])
        acc = part if acc is None else acc + part
    o_ref[...] = x + _rms(acc, gpost_ref[...])


def _mlp(x2, g_pre, wi, wo, g_post):
    t = x2.shape[0]
    tm = min(TOKEN_TILE, t)
    return pl.pallas_call(
        _mlp_kernel,
        out_shape=jax.ShapeDtypeStruct((t, D_MODEL), F32),
        grid=(t // tm,),
        in_specs=[pl.BlockSpec((tm, D_MODEL), lambda i: (i, 0)), _const_spec((1, D_MODEL)),
                  _const_spec(wi.shape), _const_spec(wo.shape), _const_spec((1, D_MODEL))],
        out_specs=pl.BlockSpec((tm, D_MODEL), lambda i: (i, 0)),
        compiler_params=_params("arbitrary"),
        name="mlp",
    )(x2, g_pre, wi, wo, g_post)


def _rot_perm(n_heads, dim):
    half = dim // 2
    return np.concatenate([h * dim + (np.arange(dim) + half) % dim for h in range(n_heads)])


def _layer_weights(w_in, mla_w_uq, mla_w_ukv):
    col = lambda a, b: w_in[:, a:b]
    w_ssd = col(_O_Z, _O_DT).astype(BF16)
    rq, rk = col(_O_RQ, _O_RK), col(_O_RK, _O_RV)
    perm = _rot_perm(RET_HEADS, RET_QK_DIM)
    w_ret = jnp.concatenate([rq, rk, rq[:, perm], rk[:, perm], col(_O_RV, _O_CQ)], axis=1).astype(BF16)
    kr = col(_O_KR, _O_GQ)
    zl = jnp.zeros((D_MODEL, MLA_PE_LANE), F32)
    zr = jnp.zeros((D_MODEL, LANES - MLA_PE_LANE - MLA_ROPE), F32)
    kperm = _rot_perm(1, MLA_ROPE)
    w_mla = jnp.concatenate([col(_O_CQ, _O_KR), zl, kr, zr, zl, kr[:, kperm], zr], axis=1).astype(BF16)
    w_gla = jnp.concatenate([col(_O_GQ, _O_GGK), col(_O_GG, _O_GATE)], axis=1).astype(BF16)
    w_small = jnp.concatenate(
        [col(_O_DT, _O_RQ), col(_O_GGK, _O_GG),
         jnp.zeros((D_MODEL, LANES - SSD_HEADS - GLA_GATE_RANK), F32)], axis=1).astype(BF16)
    w_gate = col(_O_GATE, _O_END).astype(BF16)

    hd = MLA_NOPE + MLA_ROPE
    uq = mla_w_uq.reshape(MLA_Q_LORA, MLA_HEADS, hd)
    padq = jnp.zeros((MLA_Q_LORA, MLA_HEADS, MLA_HEAD_PAD - hd), F32)
    wq = jnp.concatenate([uq, padq], axis=2).reshape(MLA_Q_LORA, -1).astype(BF16)
    uq_rot = uq[:, :, MLA_NOPE + kperm]
    wqr = jnp.concatenate([jnp.zeros((MLA_Q_LORA, MLA_HEADS, MLA_NOPE), F32), uq_rot, padq],
                          axis=2).reshape(MLA_Q_LORA, -1).astype(BF16)
    ukv = mla_w_ukv.reshape(MLA_KV_LORA, MLA_HEADS, MLA_NOPE + MLA_V)
    wk = jnp.concatenate([ukv[:, :, :MLA_NOPE],
                          jnp.zeros((MLA_KV_LORA, MLA_HEADS, MLA_HEAD_PAD - MLA_NOPE), F32)],
                         axis=2).reshape(MLA_KV_LORA, -1).astype(BF16)
    wvt = ukv[:, :, MLA_NOPE:].reshape(MLA_KV_LORA, -1).T.astype(BF16)
    return w_ssd, w_ret, w_mla, w_gla, w_small, w_gate, wq, wqr, wk, wvt


def _layer(x3, tabs, w_in, b_gate, ssd_conv_w, ssd_conv_b, ssd_dt_bias, ssd_a_log, ssd_d, ssd_norm, ret_norm,
           mla_q_norm, mla_w_uq, mla_kv_norm, mla_w_ukv, gla_w_gk2, gla_b_gk, gla_norm, w_branch, w_out,
           norm_pre_mix, norm_post_mix, norm_pre_mlp, norm_post_mlp, w_mlp_in, w_mlp_out):
    b, l, d = x3.shape
    c_ret, s_ret, c_mla, s_mla = tabs
    w_ssd, w_ret, w_mla, w_gla, w_small, w_gate, wq, wqr, wk, wvt = _layer_weights(w_in, mla_w_uq, mla_w_ukv)
    g_pre = norm_pre_mix.reshape(1, -1)
    x2 = x3.reshape(b * l, d)
    o_ssd, o_ret, o_mla, o_gla, o_small = _inproj(x2, g_pre, w_ssd, w_ret, w_mla, w_gla, w_small)
    y_ssd = _ssd(o_ssd, o_small, ssd_conv_w, ssd_conv_b, ssd_dt_bias, ssd_a_log, ssd_d, ssd_norm, b, l)
    y_ret = _ret(o_ret, c_ret, s_ret, ret_norm, b, l)
    q, k, vt = _mla_proj(o_mla, c_mla, s_mla, mla_q_norm, mla_kv_norm, wq, wqr, wk, wvt, b, l)
    y_mla_t = _flash(q, k, vt, b, l)
    y_gla = _gla(o_gla, o_small, gla_w_gk2, gla_b_gk, gla_norm, b, l)
    x3 = _merge(x3, y_ssd, y_ret, y_mla_t, y_gla, g_pre, w_gate, b_gate, w_branch.astype(BF16),
                w_out.astype(BF16), norm_post_mix.reshape(1, -1))
    x2 = _mlp(x3.reshape(b * l, d), norm_pre_mlp.reshape(1, -1), w_mlp_in.astype(BF16),
              w_mlp_out.astype(BF16), norm_post_mlp.reshape(1, -1))
    return x2.reshape(b, l, d)


def kernel(x, positions, w_in, b_gate, ssd_conv_w, ssd_conv_b, ssd_dt_bias, ssd_a_log, ssd_d, ssd_norm, ret_norm, mla_q_norm, mla_w_uq, mla_kv_norm, mla_w_ukv, gla_w_gk2, gla_b_gk, gla_norm, w_branch, w_out, norm_pre_mix, norm_post_mix, norm_pre_mlp, norm_post_mlp, w_mlp_in, w_mlp_out):
    per_layer = (w_in, b_gate, ssd_conv_w, ssd_conv_b, ssd_dt_bias, ssd_a_log, ssd_d, ssd_norm, ret_norm,
                 mla_q_norm, mla_w_uq, mla_kv_norm, mla_w_ukv, gla_w_gk2, gla_b_gk, gla_norm, w_branch, w_out,
                 norm_pre_mix, norm_post_mix, norm_pre_mlp, norm_post_mlp, w_mlp_in, w_mlp_out)
    tabs = _rope_tables(positions)
    for i in range(w_in.shape[0]):
        x = _layer(x, tabs, *(p[i] for p in per_layer))
    return x
```

```python
import functools

import numpy as np
import jax
import jax.numpy as jnp
from jax import lax
from jax.experimental import pallas as pl
from jax.experimental.pallas import tpu as pltpu

F32 = jnp.float32
BF16 = jnp.bfloat16

D_MODEL = 1024
SSD_HEADS = 8
SSD_HEAD_DIM = 64
SSD_INNER = 512
SSD_GROUPS = 2
SSD_STATE = 128
SSD_CONV = 4
SSD_CHUNK = 128
SSD_XBC = 1024
RET_HEADS = 4
RET_QK_DIM = 64
RET_V_DIM = 128
RET_CHUNK = 128
MLA_HEADS = 8
MLA_Q_LORA = 256
MLA_KV_LORA = 128
MLA_NOPE = 64
MLA_ROPE = 32
MLA_V = 64
GLA_HEADS = 4
GLA_K_DIM = 64
GLA_V_DIM = 128
GLA_GATE_RANK = 16
GLA_GATE_NORM = 16.0
GLA_CHUNK = 64
N_BRANCHES = 4
BRANCH_WIDTH = 512
D_FF = 4 * D_MODEL
ROPE_THETA = 10000.0
RMS_EPS = 1e-6

LANES = 128
MLA_HEAD_PAD = 128
VT_ROWS = MLA_V + 16
LOG2_E = 1.4426950408889634
VMEM_LIMIT = 56 * 1024 * 1024

TOKEN_TILE = 512
ATTN_BLOCK_Q = 512
ATTN_BLOCK_K = 512
ATTN_UNROLL = 8
GLA_STEP = 256
RET_ROWS = 256
RET_STEP = 512
SSD_STEP = 256

_SIZES = (SSD_INNER, SSD_XBC, SSD_HEADS,
          RET_HEADS * RET_QK_DIM, RET_HEADS * RET_QK_DIM, RET_HEADS * RET_V_DIM, RET_HEADS * RET_V_DIM,
          MLA_Q_LORA, MLA_KV_LORA, MLA_ROPE,
          GLA_HEADS * GLA_K_DIM, GLA_HEADS * GLA_K_DIM, GLA_HEADS * GLA_V_DIM, GLA_GATE_RANK,
          GLA_HEADS * GLA_V_DIM, N_BRANCHES * D_MODEL)
_OFFS = tuple(int(v) for v in np.concatenate([[0], np.cumsum(_SIZES)]))
(_O_Z, _O_XBC, _O_DT, _O_RQ, _O_RK, _O_RV, _O_RG, _O_CQ, _O_CKV, _O_KR,
 _O_GQ, _O_GK, _O_GV, _O_GGK, _O_GG, _O_GATE, _O_END) = _OFFS

SMALL_DT_LANE = 0
SMALL_GGK_LANE = 8
MLA_PE_LANE = MLA_NOPE


def _dot(a, b, precision=None):
    return jnp.dot(a, b, preferred_element_type=F32, precision=precision)


def _dot_nt(a, b):
    return lax.dot_general(a, b, (((1,), (1,)), ((), ())), preferred_element_type=F32)


def _dot_tn(a, b):
    return lax.dot_general(a, b, (((0,), (0,)), ((), ())), preferred_element_type=F32)


def _rms(x, g):
    return x * lax.rsqrt(jnp.mean(x * x, axis=-1, keepdims=True) + RMS_EPS) * g


def _silu(x):
    return x * jax.nn.sigmoid(x)


def _softplus(x):
    return jnp.maximum(x, 0.0) + jnp.log1p(jnp.exp(-jnp.abs(x)))


def _params(*sem):
    return pltpu.CompilerParams(dimension_semantics=sem, vmem_limit_bytes=VMEM_LIMIT)


def _const_spec(shape):
    nd = len(shape)
    return pl.BlockSpec(shape, lambda *_: (0,) * nd, pipeline_mode=pl.Buffered(1))


def _tril(n):
    r = lax.broadcasted_iota(jnp.int32, (n, n), 0)
    c = lax.broadcasted_iota(jnp.int32, (n, n), 1)
    return r >= c


ROPE_MLA_SRC = RET_QK_DIM // 2
ROPE_ONE_SRC = ROPE_MLA_SRC + MLA_ROPE // 2


def _rope_kernel(pos_ref, f_ref, p_ref, cr_ref, sr_ref, cm_ref, sm_ref):
    ang = pos_ref[...] * f_ref[...]
    c_hi, c_mid, _ = _split3(jnp.cos(ang))
    s_hi, s_mid, _ = _split3(jnp.sin(ang))
    tabs = _dot(jnp.concatenate([c_hi, c_mid, s_hi, s_mid], axis=1), p_ref[...])
    cr_ref[...] = tabs[:, 0:LANES]
    sr_ref[...] = tabs[:, LANES:2 * LANES]
    cm_ref[...] = tabs[:, 2 * LANES:3 * LANES]
    sm_ref[...] = tabs[:, 3 * LANES:4 * LANES]


def _rope_consts():
    half_r = RET_QK_DIM // 2
    half_m = MLA_ROPE // 2
    inv_r = ROPE_THETA ** (-jnp.arange(half_r, dtype=F32) / half_r)
    inv_m = ROPE_THETA ** (-jnp.arange(half_m, dtype=F32) / half_m)
    freq = jnp.zeros((LANES,), F32).at[0:half_r].set(inv_r).at[ROPE_MLA_SRC:ROPE_MLA_SRC + half_m].set(inv_m)
    pc = np.zeros((LANES, 4 * LANES), np.float32)
    ps = np.zeros((LANES, 4 * LANES), np.float32)
    for l in range(LANES):
        pc[l % half_r, l] = 1.0
        ps[l % half_r, LANES + l] = -1.0 if (l % RET_QK_DIM) < half_r else 1.0
        if MLA_PE_LANE <= l < MLA_PE_LANE + MLA_ROPE:
            i = l - MLA_PE_LANE
            pc[ROPE_MLA_SRC + i % half_m, 2 * LANES + l] = 1.0
            ps[ROPE_MLA_SRC + i % half_m, 3 * LANES + l] = -1.0 if i < half_m else 1.0
        else:
            pc[ROPE_ONE_SRC, 2 * LANES + l] = 1.0
    place = jnp.asarray(np.concatenate([pc, pc, ps, ps], axis=0), BF16)
    return freq.reshape(1, LANES), place


def _rope_tables(positions):
    b, l = positions.shape
    t = b * l
    tb = min(1024, t)
    pos = positions.astype(F32).reshape(t, 1)
    freq, place = _rope_consts()
    tab = jax.ShapeDtypeStruct((t, LANES), F32)
    spec = pl.BlockSpec((tb, LANES), lambda i: (i, 0))
    return pl.pallas_call(
        _rope_kernel,
        out_shape=(tab, tab, tab, tab),
        grid=(t // tb,),
        in_specs=[pl.BlockSpec((tb, 1), lambda i: (i, 0)), _const_spec((1, LANES)), _const_spec(place.shape)],
        out_specs=(spec, spec, spec, spec),
        compiler_params=_params("arbitrary"),
        name="rope_tables",
    )(pos, freq, place)


def _inproj_kernel(x_ref, g_ref, wssd_ref, wret_ref, wmla_ref, wgla_ref, wsm_ref,
                   ossd_ref, oret_ref, omla_ref, ogla_ref, osm_ref):
    h = _rms(x_ref[...], g_ref[...]).astype(BF16)
    ossd_ref[...] = _dot(h, wssd_ref[...]).astype(BF16)
    oret_ref[...] = _dot(h, wret_ref[...]).astype(BF16)
    omla_ref[...] = _dot(h, wmla_ref[...]).astype(BF16)
    ogla_ref[...] = _dot(h, wgla_ref[...]).astype(BF16)
    osm_ref[...] = _dot(h, wsm_ref[...])


def _inproj(x2, g, wssd, wret, wmla, wgla, wsm):
    t = x2.shape[0]
    tm = min(TOKEN_TILE, t)
    ws = (wssd, wret, wmla, wgla, wsm)
    outs = tuple(jax.ShapeDtypeStruct((t, w.shape[1]), BF16) for w in ws[:4]) + (
        jax.ShapeDtypeStruct((t, LANES), F32),)
    return pl.pallas_call(
        _inproj_kernel,
        out_shape=outs,
        grid=(t // tm,),
        in_specs=[pl.BlockSpec((tm, D_MODEL), lambda i: (i, 0)), _const_spec((1, D_MODEL))]
                 + [_const_spec(w.shape) for w in ws],
        out_specs=tuple(pl.BlockSpec((tm, o.shape[1]), lambda i: (i, 0)) for o in outs),
        compiler_params=_params("arbitrary"),
        name="in_proj",
    )(x2, g, *ws)


def _split3(x):
    hi = x.astype(BF16)
    r = x - hi.astype(F32)
    mid = r.astype(BF16)
    lo = (r - mid.astype(F32)).astype(BF16)
    return hi, mid, lo


def _dot_exact01(a01, x):
    n = x.shape[1]
    parts = _dot(a01, jnp.concatenate(_split3(x), axis=1))
    return parts[:, 0:n] + parts[:, n:2 * n] + parts[:, 2 * n:3 * n]


def _chunk_tril(rows, chunk):
    r = lax.broadcasted_iota(jnp.int32, (rows, rows), 0)
    c = lax.broadcasted_iota(jnp.int32, (rows, rows), 1)
    shift = chunk.bit_length() - 1
    return ((r >> shift) == (c >> shift)) & (r >= c)


def _ssd_kernel(blk_ref, sm_ref, cw_ref, cb_ref, shift_ref, dtb_ref, alog_ref, exp_ref, dsk_ref, ng_ref,
                y_ref, conv_ref, state_ref):
    q = SSD_CHUNK
    rows = blk_ref.shape[0]
    n_chunks = rows // q
    tail = 8

    @pl.when(pl.program_id(1) == 0)
    def _():
        conv_ref[0:tail, :] = jnp.zeros((tail, SSD_XBC), F32)
        state_ref[...] = jnp.zeros_like(state_ref)

    z = blk_ref[:, 0:SSD_INNER].astype(F32)
    xbc_b = blk_ref[:, SSD_INNER:SSD_INNER + SSD_XBC]
    xbc = xbc_b.astype(F32)
    conv_ref[tail:2 * tail, :] = xbc[0:tail, :]
    acc = cb_ref[...] + cw_ref[SSD_CONV - 1:SSD_CONV, :] * xbc
    head = acc[0:tail, :]
    for j in range(SSD_CONV - 1):
        off = tail - (SSD_CONV - 1) + j
        acc = acc + cw_ref[j:j + 1, :] * _dot(shift_ref[j], xbc_b)
        head = head + cw_ref[j:j + 1, :] * conv_ref[off:off + tail, :]
    conv_ref[0:tail, :] = xbc[rows - tail:rows, :]
    xbc = _silu(jnp.concatenate([head, acc[tail:, :]], axis=0))
    xs = xbc[:, 0:SSD_INNER]
    gn = SSD_GROUPS * SSD_STATE
    bm = xbc[:, SSD_INNER:SSD_INNER + gn].astype(BF16)
    cm = xbc[:, SSD_INNER + gn:SSD_INNER + 2 * gn].astype(BF16)

    dt = _softplus(sm_ref[...] + dtb_ref[...])
    la = dt * (-jnp.exp(alog_ref[...]))
    tri = jnp.where(_chunk_tril(rows, q), 1.0, 0.0).astype(BF16)
    cum = _dot_exact01(tri, la)
    expand3 = exp_ref[...]
    cum_ch = _dot(jnp.concatenate(_split3(cum), axis=1), expand3)
    dt_ch = _dot(jnp.concatenate(_split3(dt), axis=1), expand3)
    xdt = xs * dt_ch
    xdt_b = xdt.astype(BF16)
    last = jnp.concatenate(
        [jnp.broadcast_to(cum_ch[(j + 1) * q - 1:(j + 1) * q, :], (q, SSD_INNER)) for j in range(n_chunks)], axis=0)
    xdte_b = (xdt * jnp.exp(last - cum_ch)).astype(BF16)

    hpg = SSD_HEADS // SSD_GROUPS
    gw = hpg * SSD_HEAD_DIM
    lane_lo = lax.broadcasted_iota(jnp.int32, (q, LANES), 1) < SSD_HEAD_DIM
    mask = _tril(q)
    state = [state_ref[g] for g in range(SSD_GROUPS)]
    y_rows = []
    for c in range(n_chunks):
        sl = slice(c * q, (c + 1) * q)
        cum_c = cum[sl]
        cum_t = cum_c.T
        chunk_decay = jnp.exp(cum_ch[(c + 1) * q - 1:(c + 1) * q, :])
        ydiag, yoff = [], []
        for g in range(SSD_GROUPS):
            cg = cm[sl, g * SSD_STATE:(g + 1) * SSD_STATE]
            bg = bm[sl, g * SSD_STATE:(g + 1) * SSD_STATE]
            s = _dot_nt(cg, bg)
            yoff.append(_dot(cg, state[g].astype(BF16)))
            for j in range(hpg // 2):
                xpair = xdt_b[sl, g * gw + j * LANES:g * gw + (j + 1) * LANES]
                ys = []
                for e in range(2):
                    h = g * hpg + 2 * j + e
                    seg = cum_c[:, h:h + 1] - cum_t[h:h + 1, :]
                    a = (s * jnp.exp(jnp.where(mask, seg, -jnp.inf))).astype(BF16)
                    ys.append(_dot(a, xpair))
                ydiag.append(jnp.where(lane_lo, ys[0], ys[1]))
            state[g] = (state[g] * chunk_decay[:, g * gw:(g + 1) * gw]
                        + _dot_tn(bg, xdte_b[sl, g * gw:(g + 1) * gw]))
        y_rows.append(jnp.concatenate(ydiag, axis=1) + jnp.concatenate(yoff, axis=1) * jnp.exp(cum_ch[sl]))
    for g in range(SSD_GROUPS):
        state_ref[g] = state[g]
    y = jnp.concatenate(y_rows, axis=0)
    y = (y + dsk_ref[...] * xs) * _silu(z)
    outs = []
    for g in range(SSD_GROUPS):
        outs.append(_rms(y[:, g * gw:(g + 1) * gw], ng_ref[:, g * gw:(g + 1) * gw]))
    y_ref[...] = jnp.concatenate(outs, axis=1).astype(BF16)


def _ssd(o_ssd, o_small, conv_w, conv_b, dt_bias, a_log, d_skip, norm_g, b, l):
    rows = min(SSD_STEP, l)
    nc = l // rows
    pad = LANES - SSD_HEADS
    dtb = jnp.pad(dt_bias, (0, pad)).reshape(1, LANES)
    alog = jnp.pad(a_log, (0, pad)).reshape(1, LANES)
    head_of = np.arange(SSD_INNER) // SSD_HEAD_DIM
    expand = np.arange(LANES)[:, None] == head_of[None, :]
    expand3 = jnp.asarray(np.concatenate([expand] * 3, axis=0), BF16)
    dsk = jnp.repeat(d_skip, SSD_HEAD_DIM).reshape(1, SSD_INNER)
    t = np.arange(rows)
    shifts = jnp.asarray(np.stack([t[None, :] == t[:, None] - (SSD_CONV - 1 - j) for j in range(SSD_CONV - 1)]), BF16)
    row = lambda bi, ci: (bi * nc + ci, 0)
    return pl.pallas_call(
        _ssd_kernel,
        out_shape=jax.ShapeDtypeStruct((b * l, SSD_INNER), BF16),
        grid=(b, nc),
        in_specs=[pl.BlockSpec((rows, SSD_INNER + SSD_XBC), row), pl.BlockSpec((rows, LANES), row),
                  _const_spec((SSD_CONV, SSD_XBC)), _const_spec((1, SSD_XBC)), _const_spec(shifts.shape),
                  _const_spec((1, LANES)), _const_spec((1, LANES)), _const_spec(expand3.shape),
                  _const_spec((1, SSD_INNER)), _const_spec((1, SSD_INNER))],
        out_specs=pl.BlockSpec((rows, SSD_INNER), row),
        scratch_shapes=[pltpu.VMEM((16, SSD_XBC), F32),
                        pltpu.VMEM((SSD_GROUPS, SSD_STATE, SSD_INNER // SSD_GROUPS), F32)],
        compiler_params=_params("arbitrary", "arbitrary"),
        name="ssd_mixer",
    )(o_ssd, o_small, conv_w, conv_b.reshape(1, -1), shifts, dtb, alog, expand3, dsk, norm_g.reshape(1, -1))


def _head_rows(full, n_heads, rows, cols):
    return jnp.concatenate([full[h * rows:(h + 1) * rows, h * cols:(h + 1) * cols] for h in range(n_heads)], axis=0)


def _ret_kernel(blk_ref, c_ref, s_ref, dm_ref, qs_ref, ks_ref, cd_ref, ng_ref, y_ref, state_ref):
    @pl.when(pl.program_id(1) == 0)
    def _():
        state_ref[...] = jnp.zeros_like(state_ref)

    w = RET_HEADS * RET_QK_DIM
    vw = RET_HEADS * RET_V_DIM
    rows = dm_ref.shape[1]
    lane = lax.broadcasted_iota(jnp.int32, (1, w), 1)
    first_half = (lane & (RET_QK_DIM - 1)) < RET_QK_DIM // 2
    state = state_ref[...]
    for j in range(blk_ref.shape[0] // rows):
        sl = slice(j * rows, (j + 1) * rows)
        ct = jnp.concatenate([c_ref[sl, :]] * (w // LANES), axis=1)
        st = jnp.concatenate([s_ref[sl, :]] * (w // LANES), axis=1)

        def rope(x):
            partner = jnp.where(first_half, pltpu.roll(x, w - RET_QK_DIM // 2, axis=1),
                                pltpu.roll(x, RET_QK_DIM // 2, axis=1))
            return x * ct + partner * st

        q = rope(blk_ref[sl, 0:w].astype(F32))
        k = rope(blk_ref[sl, w:2 * w].astype(F32)) * (RET_QK_DIM ** -0.5)
        v = blk_ref[sl, 2 * w:2 * w + vw]
        g = blk_ref[sl, 2 * w + vw:2 * w + 2 * vw].astype(F32)
        k_b = k.astype(BF16)
        qo = q * qs_ref[...]
        state_b = state.astype(BF16)
        ys = []
        for h in range(RET_HEADS):
            in_head = (lane >= h * RET_QK_DIM) & (lane < (h + 1) * RET_QK_DIM)
            qh = jnp.where(in_head, q, 0.0).astype(BF16)
            sc = (_dot_nt(qh, k_b) * dm_ref[h]).astype(BF16)
            yh = (_dot(sc, v[:, h * RET_V_DIM:(h + 1) * RET_V_DIM])
                  + _dot(jnp.where(in_head, qo, 0.0).astype(BF16), state_b))
            ys.append(_rms(yh, ng_ref[:, h * RET_V_DIM:(h + 1) * RET_V_DIM]))
        upd = _head_rows(_dot_tn((k * ks_ref[...]).astype(BF16), v), RET_HEADS, RET_QK_DIM, RET_V_DIM)
        state = state * cd_ref[...] + upd
        y_ref[sl, :] = (jnp.concatenate(ys, axis=1) * _silu(g)).astype(BF16)
    state_ref[...] = state


def _ret_consts(rows):
    lg = np.log1p(-np.exp2(-5.0 - np.arange(RET_HEADS, dtype=np.float64)))
    i = np.arange(rows)
    dm = np.where(i[:, None] >= i[None, :], np.exp(lg[:, None, None] * (i[:, None] - i[None, :])[None]), 0.0)
    qs = np.repeat(np.exp(lg[None, :] * (i[:, None] + 1)), RET_QK_DIM, axis=1)
    ks = np.repeat(np.exp(lg[None, :] * (rows - 1 - i[:, None])), RET_QK_DIM, axis=1)
    cd = np.broadcast_to(np.repeat(np.exp(lg * rows), RET_QK_DIM)[:, None], (RET_HEADS * RET_QK_DIM, RET_V_DIM))
    return tuple(jnp.asarray(a, F32) for a in (dm, qs, ks, cd))


def _ret(o_ret, c_tab, s_tab, norm_g, b, l):
    rows = min(RET_STEP, l)
    nc = l // rows
    w = RET_HEADS * RET_QK_DIM
    vw = RET_HEADS * RET_V_DIM
    dm, qs, ks, cd = _ret_consts(min(RET_ROWS, rows))
    row = lambda bi, ci: (bi * nc + ci, 0)
    stream = lambda width: pl.BlockSpec((rows, width), row)
    return pl.pallas_call(
        _ret_kernel,
        out_shape=jax.ShapeDtypeStruct((b * l, vw), BF16),
        grid=(b, nc),
        in_specs=[stream(2 * w + 2 * vw), stream(LANES), stream(LANES),
                  _const_spec(dm.shape), _const_spec(qs.shape), _const_spec(ks.shape),
                  _const_spec(cd.shape), _const_spec((1, vw))],
        out_specs=pl.BlockSpec((rows, vw), row),
        scratch_shapes=[pltpu.VMEM((w, RET_V_DIM), F32)],
        compiler_params=_params("arbitrary", "arbitrary"),
        name="ret_mixer",
    )(o_ret, c_tab, s_tab, dm, qs, ks, cd, norm_g.reshape(1, -1))


def _gla_kernel(blk_ref, sm_ref, w2_ref, bgk_ref, ng_ref, y_ref, state_ref):
    @pl.when(pl.program_id(1) == 0)
    def _():
        state_ref[...] = jnp.zeros_like(state_ref)

    c = GLA_CHUNK
    rows = blk_ref.shape[0]
    kw = GLA_HEADS * GLA_K_DIM
    vw = GLA_HEADS * GLA_V_DIM
    sm_hi, sm_mid, _ = _split3(sm_ref[...])
    logits = _dot(jnp.concatenate([sm_hi, sm_hi, sm_mid], axis=1), w2_ref[...]) + bgk_ref[...]
    log_g = -_softplus(-logits) * (1.0 / GLA_GATE_NORM)
    tri = _chunk_tril(rows, c)
    cum = _dot_exact01(jnp.where(tri, 1.0, 0.0).astype(BF16), log_g)
    n_chunks = rows // c
    total = jnp.concatenate(
        [jnp.broadcast_to(cum[(j + 1) * c - 1:(j + 1) * c, :], (c, kw)) for j in range(n_chunks)], axis=0)
    q = blk_ref[:, 0:kw].astype(F32) * (GLA_K_DIM ** -0.5)
    k = blk_ref[:, kw:2 * kw].astype(F32)
    v = blk_ref[:, 2 * kw:2 * kw + vw]
    g = blk_ref[:, 2 * kw + vw:2 * kw + 2 * vw].astype(F32)
    q_in = q * jnp.exp(cum)
    k_in = (k * jnp.exp(-cum)).astype(BF16)
    k_st = (k * jnp.exp(total - cum)).astype(BF16)
    lane = lax.broadcasted_iota(jnp.int32, (1, kw), 1)
    in_head = [(lane >= h * GLA_K_DIM) & (lane < (h + 1) * GLA_K_DIM) for h in range(GLA_HEADS)]

    state = state_ref[...]
    o_inter = []
    for j in range(n_chunks):
        sl = slice(j * c, (j + 1) * c)
        q_stack = jnp.concatenate([jnp.where(m, q_in[sl], 0.0) for m in in_head], axis=0).astype(BF16)
        k_stack = jnp.concatenate([jnp.where(m, k_st[sl], 0) for m in in_head], axis=0)
        v_stack = jnp.concatenate([v[sl, h * GLA_V_DIM:(h + 1) * GLA_V_DIM] for h in range(GLA_HEADS)], axis=0)
        oi = _dot(q_stack, state.astype(BF16))
        o_inter.append(jnp.concatenate([oi[h * c:(h + 1) * c] for h in range(GLA_HEADS)], axis=1))
        last = cum[(j + 1) * c - 1:(j + 1) * c, :]
        decay = jnp.exp(jnp.broadcast_to(last, (LANES, kw)).T)
        state = state * decay + _dot_tn(k_stack, v_stack)
    state_ref[...] = state
    o_inter = jnp.concatenate(o_inter, axis=0)
    outs = []
    for h in range(GLA_HEADS):
        qh = jnp.where(in_head[h], q_in, 0.0).astype(BF16)
        sc = jnp.where(tri, _dot_nt(qh, k_in), 0.0).astype(BF16)
        oh = _dot(sc, v[:, h * GLA_V_DIM:(h + 1) * GLA_V_DIM]) + o_inter[:, h * GLA_V_DIM:(h + 1) * GLA_V_DIM]
        outs.append(_rms(oh, ng_ref[:, h * GLA_V_DIM:(h + 1) * GLA_V_DIM]))
    y_ref[...] = (jnp.concatenate(outs, axis=1) * _silu(g)).astype(BF16)


def _gla(o_gla, o_small, w_gk2, b_gk, norm_g, b, l):
    rows = min(GLA_STEP, l)
    ns = l // rows
    kw = GLA_HEADS * GLA_K_DIM
    vw = GLA_HEADS * GLA_V_DIM
    w2 = jnp.zeros((LANES, kw), F32).at[SMALL_GGK_LANE:SMALL_GGK_LANE + GLA_GATE_RANK].set(w_gk2)
    w2_hi = w2.astype(BF16)
    w2_mid = (w2 - w2_hi.astype(F32)).astype(BF16)
    w2 = jnp.concatenate([w2_hi, w2_mid, w2_hi], axis=0)
    row = lambda bi, ci: (bi * ns + ci, 0)
    return pl.pallas_call(
        _gla_kernel,
        out_shape=jax.ShapeDtypeStruct((b * l, vw), BF16),
        grid=(b, ns),
        in_specs=[pl.BlockSpec((rows, 2 * kw + 2 * vw), row), pl.BlockSpec((rows, LANES), row),
                  _const_spec(w2.shape), _const_spec((1, kw)), _const_spec((1, vw))],
        out_specs=pl.BlockSpec((rows, vw), row),
        scratch_shapes=[pltpu.VMEM((kw, GLA_V_DIM), F32)],
        compiler_params=_params("arbitrary", "arbitrary"),
        name="gla_mixer",
    )(o_gla, o_small, w2, b_gk.reshape(1, -1), norm_g.reshape(1, -1))


def _mla_proj_kernel(blk_ref, c_ref, s_ref, qn_ref, kvn_ref, wq_ref, wqr_ref, wk_ref, wvt_ref,
                     q_ref, k_ref, vt_ref):
    o_ckv = MLA_Q_LORA
    o_kr = o_ckv + MLA_KV_LORA
    cq = _rms(blk_ref[:, 0:MLA_Q_LORA].astype(F32), qn_ref[...]).astype(BF16)
    ckv = _rms(blk_ref[:, o_ckv:o_kr].astype(F32), kvn_ref[...]).astype(BF16)
    c = c_ref[...]
    s = s_ref[...]
    ct = jnp.concatenate([c] * MLA_HEADS, axis=1)
    st = jnp.concatenate([s] * MLA_HEADS, axis=1)
    scale = (MLA_NOPE + MLA_ROPE) ** -0.5 * LOG2_E
    q = (_dot(cq, wq_ref[...]) * ct + _dot(cq, wqr_ref[...]) * st) * scale
    q_ref[...] = q.astype(BF16)
    kr = blk_ref[:, o_kr:o_kr + LANES].astype(F32)
    lane = lax.broadcasted_iota(jnp.int32, (1, LANES), 1)
    half = MLA_ROPE // 2
    kpe = kr * c + jnp.where(lane < MLA_PE_LANE + half, pltpu.roll(kr, LANES - half, axis=1), pltpu.roll(kr, half, axis=1)) * s
    k = _dot(ckv, wk_ref[...]) + jnp.concatenate([kpe] * MLA_HEADS, axis=1)
    k_ref[...] = k.astype(BF16)
    vt = _dot_nt(wvt_ref[...], ckv).astype(BF16)
    ones = jnp.ones((VT_ROWS - MLA_V, vt.shape[1]), BF16)
    for h in range(MLA_HEADS):
        vt_ref[h, 0:MLA_V, :] = vt[h * MLA_V:(h + 1) * MLA_V, :]
        vt_ref[h, MLA_V:VT_ROWS, :] = ones


def _mla_proj(o_mla, c_tab, s_tab, q_norm, kv_norm, wq, wqr, wk, wvt, b, l):
    tm = min(ATTN_BLOCK_K, l)
    nk = l // tm
    hw = MLA_HEADS * MLA_HEAD_PAD
    row = lambda bi, j: (bi * nk + j, 0)
    return pl.pallas_call(
        _mla_proj_kernel,
        out_shape=(jax.ShapeDtypeStruct((b * l, hw), BF16), jax.ShapeDtypeStruct((b * l, hw), BF16),
                   jax.ShapeDtypeStruct((b, nk, MLA_HEADS, VT_ROWS, tm), BF16)),
        grid=(b, nk),
        in_specs=[pl.BlockSpec((tm, o_mla.shape[1]), row), pl.BlockSpec((tm, LANES), row),
                  pl.BlockSpec((tm, LANES), row),
                  _const_spec((1, MLA_Q_LORA)), _const_spec((1, MLA_KV_LORA)),
                  _const_spec(wq.shape), _const_spec(wqr.shape), _const_spec(wk.shape), _const_spec(wvt.shape)],
        out_specs=(pl.BlockSpec((tm, hw), row), pl.BlockSpec((tm, hw), row),
                   pl.BlockSpec((None, None, MLA_HEADS, VT_ROWS, tm), lambda bi, j: (bi, j, 0, 0, 0))),
        compiler_params=_params("arbitrary", "arbitrary"),
        name="mla_proj",
    )(o_mla, c_tab, s_tab, q_norm.reshape(1, -1), kv_norm.reshape(1, -1), wq, wqr, wk, wvt)


def _flash_kernel(nq, tq_ref, tk_ref, q_ref, k_ref, vt_ref, o_ref, m_ref, acc_ref, s_ref, mb0_ref, mb1_ref):
    bq, bk = ATTN_BLOCK_Q, ATTN_BLOCK_K
    half = bq // 2
    n_off = nq * (nq - 1) // 2
    n_blk = n_off + nq
    m_ref[...] = jnp.full(m_ref.shape, -jnp.inf, F32)
    acc_ref[...] = jnp.zeros(acc_ref.shape, F32)
    mb_refs = (mb0_ref, mb1_ref)

    def produce(t, slot, diagonal):
        q = q_ref[pl.ds(pl.multiple_of(tq_ref[t] * bq, bq), bq), :]
        k = k_ref[pl.ds(pl.multiple_of(tk_ref[t] * bk, bk), bk), :]
        s = _dot_nt(k, q)
        if diagonal:
            kpos = lax.broadcasted_iota(jnp.int32, (bk, bq), 0)
            qpos = lax.broadcasted_iota(jnp.int32, (bk, bq), 1)
            s = jnp.where(kpos <= qpos, s, -jnp.inf)
        s_ref[0:bk, slot * half:(slot + 1) * half] = s[:, 0:half]
        s_ref[bk:2 * bk, slot * half:(slot + 1) * half] = s[:, half:bq]
        mb_refs[slot][...] = jnp.max(s, axis=0, keepdims=True)

    def consume(t, slot, diagonal):
        qi = tq_ref[t]
        m_old = m_ref[qi]
        m_new = jnp.maximum(m_old, mb_refs[slot][...])
        alpha = jnp.exp2(m_old - m_new)
        s = jnp.concatenate([s_ref[0:bk, slot * half:(slot + 1) * half],
                             s_ref[bk:2 * bk, slot * half:(slot + 1) * half]], axis=1)
        p = jnp.exp2(s - m_new).astype(BF16)
        acc = acc_ref[qi] * alpha + _dot(vt_ref[tk_ref[t]], p)
        if diagonal:
            o_ref[qi] = (acc[0:MLA_V, :] / acc[MLA_V:MLA_V + 1, :]).astype(BF16)
        else:
            acc_ref[qi] = acc
            m_ref[qi] = m_new

    def static_steps(t_from, t_to):
        for t in range(t_from, t_to):
            if t + 1 < n_blk:
                produce(t + 1, (t + 1) % 2, t + 1 >= n_off)
            consume(t, t % 2, t >= n_off)

    u = ATTN_UNROLL

    def group_loop(t_from, n_groups, diagonal):
        def group(i, carry):
            t0 = t_from + u * i
            for d in range(u):
                produce(t0 + d + 1, (d + 1) % 2, diagonal)
                consume(t0 + d, d % 2, diagonal)
            return carry

        if n_groups > 0:
            assert t_from % 2 == 0 and u % 2 == 0
            lax.fori_loop(0, n_groups, group, 0)

    produce(0, 0, n_off == 0)
    n_a = max(0, (n_off - 1) // u)
    group_loop(0, n_a, False)
    t_b = min(n_off + n_off % 2, n_blk)
    static_steps(u * n_a, t_b)
    n_b = max(0, (n_blk - 1 - t_b) // u)
    group_loop(t_b, n_b, True)
    static_steps(t_b + u * n_b, n_blk)


def _flash_order(nq):
    off = [(qi, kb) for qi in range(nq) for kb in range(qi)]
    blocks = off + [(qi, qi) for qi in range(nq)]
    return (jnp.asarray([b[0] for b in blocks], jnp.int32), jnp.asarray([b[1] for b in blocks], jnp.int32))


def _flash(q, k, vt, b, l):
    bq, bk = ATTN_BLOCK_Q, ATTN_BLOCK_K
    assert bq == bk
    nq = l // bq
    nk = l // bk
    q3 = q.reshape(b, l, MLA_HEADS * MLA_HEAD_PAD)
    k3 = k.reshape(b, l, MLA_HEADS * MLA_HEAD_PAD)
    tq, tk = _flash_order(nq)
    grid_spec = pltpu.PrefetchScalarGridSpec(
        num_scalar_prefetch=2,
        grid=(b, MLA_HEADS),
        in_specs=[pl.BlockSpec((None, l, MLA_HEAD_PAD), lambda bi, h, *_: (bi, 0, h)),
                  pl.BlockSpec((None, l, MLA_HEAD_PAD), lambda bi, h, *_: (bi, 0, h)),
                  pl.BlockSpec((None, nk, None, VT_ROWS, bk), lambda bi, h, *_: (bi, 0, h, 0, 0))],
        out_specs=pl.BlockSpec((None, None, nq, MLA_V, bq), lambda bi, h, *_: (bi, h, 0, 0, 0)),
        scratch_shapes=[pltpu.VMEM((nq, 1, bq), F32), pltpu.VMEM((nq, VT_ROWS, bq), F32),
                        pltpu.VMEM((2 * bk, bq), F32),
                        pltpu.VMEM((1, bq), F32), pltpu.VMEM((1, bq), F32)])
    return pl.pallas_call(
        functools.partial(_flash_kernel, nq),
        out_shape=jax.ShapeDtypeStruct((b, MLA_HEADS, nq, MLA_V, bq), BF16),
        grid_spec=grid_spec,
        compiler_params=_params("arbitrary", "arbitrary"),
        name="mla_flash",
    )(tq, tk, q3, k3, vt)


def _merge_kernel(x_ref, yssd_ref, yret_ref, ymt_ref, ygla_ref, gpre_ref, wg_ref, bg_ref, wb_ref, wo_ref,
                  gpost_ref, o_ref):
    x = x_ref[...]
    h = _rms(x, gpre_ref[...]).astype(BF16)
    branches = (_dot(yssd_ref[...], wb_ref[0]), _dot(yret_ref[...], wb_ref[1]),
                _dot_tn(ymt_ref[...].reshape(BRANCH_WIDTH, -1), wb_ref[2]), _dot(ygla_ref[...], wb_ref[3]))
    merged = None
    for i, br in enumerate(branches):
        gate = jax.nn.sigmoid(_dot(h, wg_ref[:, i * D_MODEL:(i + 1) * D_MODEL]) + bg_ref[:, i * D_MODEL:(i + 1) * D_MODEL])
        merged = gate * br if merged is None else merged + gate * br
    o = _dot(merged.astype(BF16), wo_ref[...])
    o_ref[...] = x + _rms(o, gpost_ref[...])


def _merge(x3, y_ssd, y_ret, y_mla_t, y_gla, g_pre, wg, b_gate, wb, wo, g_post):
    b, l, _ = x3.shape
    tm = min(TOKEN_TILE, l)
    assert tm == y_mla_t.shape[-1]
    nt = l // tm
    w = BRANCH_WIDTH
    tok = lambda width: pl.BlockSpec((None, tm, width), lambda bi, j: (bi, j, 0))
    return pl.pallas_call(
        _merge_kernel,
        out_shape=jax.ShapeDtypeStruct((b, l, D_MODEL), F32),
        grid=(b, nt),
        in_specs=[tok(D_MODEL), tok(w), tok(w),
                  pl.BlockSpec((None, MLA_HEADS, None, MLA_V, tm), lambda bi, j: (bi, 0, j, 0, 0)), tok(w),
                  _const_spec((1, D_MODEL)), _const_spec(wg.shape), _const_spec((1, N_BRANCHES * D_MODEL)),
                  _const_spec(wb.shape), _const_spec(wo.shape), _const_spec((1, D_MODEL))],
        out_specs=tok(D_MODEL),
        compiler_params=_params("arbitrary", "arbitrary"),
        name="merge_out",
    )(x3, y_ssd.reshape(b, l, w), y_ret.reshape(b, l, w), y_mla_t, y_gla.reshape(b, l, w),
      g_pre, wg, b_gate.reshape(1, -1), wb, wo, g_post)


def _mlp_kernel(x_ref, gpre_ref, wi_ref, wo_ref, gpost_ref, o_ref):
    x = x_ref[...]
    h = _rms(x, gpre_ref[...]).astype(BF16)
    acc = None
    for j in range(D_FF // D_MODEL):
        u = jnp.maximum(_dot(h, wi_ref[:, j * D_MODEL:(j + 1) * D_MODEL]), 0.0)
        part = _dot((u * u).astype(BF16), wo_ref[j * D_MODEL:(j + 1) * D_MODEL, :])
        acc = part if acc is None else acc + part
    o_ref[...] = x + _rms(acc, gpost_ref[...])


def _mlp(x2, g_pre, wi, wo, g_post):
    t = x2.shape[0]
    tm = min(TOKEN_TILE, t)
    return pl.pallas_call(
        _mlp_kernel,
        out_shape=jax.ShapeDtypeStruct((t, D_MODEL), F32),
        grid=(t // tm,),
        in_specs=[pl.BlockSpec((tm, D_MODEL), lambda i: (i, 0)), _const_spec((1, D_MODEL)),
                  _const_spec(wi.shape), _const_spec(wo.shape), _const_spec((1, D_MODEL))],
        out_specs=pl.BlockSpec((tm, D_MODEL), lambda i: (i, 0)),
        compiler_params=_params("arbitrary"),
        name="mlp",
    )(x2, g_pre, wi, wo, g_post)


def _rot_perm(n_heads, dim):
    half = dim // 2
    return np.concatenate([h * dim + (np.arange(dim) + half) % dim for h in range(n_heads)])


def _layer_weights(w_in, mla_w_uq, mla_w_ukv):
    col = lambda a, b: w_in[:, a:b]
    w_ssd = col(_O_Z, _O_DT).astype(BF16)
    w_ret = col(_O_RQ, _O_CQ).astype(BF16)
    zl = jnp.zeros((D_MODEL, MLA_PE_LANE), F32)
    zr = jnp.zeros((D_MODEL, LANES - MLA_PE_LANE - MLA_ROPE), F32)
    kperm = _rot_perm(1, MLA_ROPE)
    w_mla = jnp.concatenate([col(_O_CQ, _O_KR), zl, col(_O_KR, _O_GQ), zr], axis=1).astype(BF16)
    w_gla = jnp.concatenate([col(_O_GQ, _O_GGK), col(_O_GG, _O_GATE)], axis=1).astype(BF16)
    w_small = jnp.concatenate(
        [col(_O_DT, _O_RQ), col(_O_GGK, _O_GG),
         jnp.zeros((D_MODEL, LANES - SSD_HEADS - GLA_GATE_RANK), F32)], axis=1).astype(BF16)
    w_gate = col(_O_GATE, _O_END).astype(BF16)

    hd = MLA_NOPE + MLA_ROPE
    uq = mla_w_uq.reshape(MLA_Q_LORA, MLA_HEADS, hd)
    padq = jnp.zeros((MLA_Q_LORA, MLA_HEADS, MLA_HEAD_PAD - hd), F32)
    wq = jnp.concatenate([uq, padq], axis=2).reshape(MLA_Q_LORA, -1).astype(BF16)
    uq_rot = uq[:, :, MLA_NOPE + kperm]
    wqr = jnp.concatenate([jnp.zeros((MLA_Q_LORA, MLA_HEADS, MLA_NOPE), F32), uq_rot, padq],
                          axis=2).reshape(MLA_Q_LORA, -1).astype(BF16)
    ukv = mla_w_ukv.reshape(MLA_KV_LORA, MLA_HEADS, MLA_NOPE + MLA_V)
    wk = jnp.concatenate([ukv[:, :, :MLA_NOPE],
                          jnp.zeros((MLA_KV_LORA, MLA_HEADS, MLA_HEAD_PAD - MLA_NOPE), F32)],
                         axis=2).reshape(MLA_KV_LORA, -1).astype(BF16)
    wvt = ukv[:, :, MLA_NOPE:].reshape(MLA_KV_LORA, -1).T.astype(BF16)
    return w_ssd, w_ret, w_mla, w_gla, w_small, w_gate, wq, wqr, wk, wvt


def _layer(x3, tabs, w_in, b_gate, ssd_conv_w, ssd_conv_b, ssd_dt_bias, ssd_a_log, ssd_d, ssd_norm, ret_norm,
           mla_q_norm, mla_w_uq, mla_kv_norm, mla_w_ukv, gla_w_gk2, gla_b_gk, gla_norm, w_branch, w_out,
           norm_pre_mix, norm_post_mix, norm_pre_mlp, norm_post_mlp, w_mlp_in, w_mlp_out):
    b, l, d = x3.shape
    c_ret, s_ret, c_mla, s_mla = tabs
    w_ssd, w_ret, w_mla, w_gla, w_small, w_gate, wq, wqr, wk, wvt = _layer_weights(w_in, mla_w_uq, mla_w_ukv)
    g_pre = norm_pre_mix.reshape(1, -1)
    x2 = x3.reshape(b * l, d)
    o_ssd, o_ret, o_mla, o_gla, o_small = _inproj(x2, g_pre, w_ssd, w_ret, w_mla, w_gla, w_small)
    y_ssd = _ssd(o_ssd, o_small, ssd_conv_w, ssd_conv_b, ssd_dt_bias, ssd_a_log, ssd_d, ssd_norm, b, l)
    y_ret = _ret(o_ret, c_ret, s_ret, ret_norm, b, l)
    q, k, vt = _mla_proj(o_mla, c_mla, s_mla, mla_q_norm, mla_kv_norm, wq, wqr, wk, wvt, b, l)
    y_mla_t = _flash(q, k, vt, b, l)
    y_gla = _gla(o_gla, o_small, gla_w_gk2, gla_b_gk, gla_norm, b, l)
    x3 = _merge(x3, y_ssd, y_ret, y_mla_t, y_gla, g_pre, w_gate, b_gate, w_branch.astype(BF16),
                w_out.astype(BF16), norm_post_mix.reshape(1, -1))
    x2 = _mlp(x3.reshape(b * l, d), norm_pre_mlp.reshape(1, -1), w_mlp_in.astype(BF16),
              w_mlp_out.astype(BF16), norm_post_mlp.reshape(1, -1))
    return x2.reshape(b, l, d)


def kernel(x, positions, w_in, b_gate, ssd_conv_w, ssd_conv_b, ssd_dt_bias, ssd_a_log, ssd_d, ssd_norm, ret_norm, mla_q_norm, mla_w_uq, mla_kv_norm, mla_w_ukv, gla_w_gk2, gla_b_gk, gla_norm, w_branch, w_out, norm_pre_mix, norm_post_mix, norm_pre_mlp, norm_post_mlp, w_mlp_in, w_mlp_out):
    per_layer = (w_in, b_gate, ssd_conv_w, ssd_conv_b, ssd_dt_bias, ssd_a_log, ssd_d, ssd_norm, ret_norm,
                 mla_q_norm, mla_w_uq, mla_kv_norm, mla_w_ukv, gla_w_gk2, gla_b_gk, gla_norm, w_branch, w_out,
                 norm_pre_mix, norm_post_mix, norm_pre_mlp, norm_post_mlp, w_mlp_in, w_mlp_out)
    tabs = _rope_tables(positions)
    for i in range(w_in.shape[0]):
        x = _layer(x, tabs, *(p[i] for p in per_layer))
    return x
```

```python
import functools

import numpy as np
import jax
import jax.numpy as jnp
from jax import lax
from jax.experimental import pallas as pl
from jax.experimental.pallas import tpu as pltpu

F32 = jnp.float32
BF16 = jnp.bfloat16

D_MODEL = 1024
SSD_HEADS = 8
SSD_HEAD_DIM = 64
SSD_INNER = 512
SSD_GROUPS = 2
SSD_STATE = 128
SSD_CONV = 4
SSD_CHUNK = 128
SSD_XBC = 1024
RET_HEADS = 4
RET_QK_DIM = 64
RET_V_DIM = 128
RET_CHUNK = 128
MLA_HEADS = 8
MLA_Q_LORA = 256
MLA_KV_LORA = 128
MLA_NOPE = 64
MLA_ROPE = 32
MLA_V = 64
GLA_HEADS = 4
GLA_K_DIM = 64
GLA_V_DIM = 128
GLA_GATE_RANK = 16
GLA_GATE_NORM = 16.0
GLA_CHUNK = 64
N_BRANCHES = 4
BRANCH_WIDTH = 512
D_FF = 4 * D_MODEL
ROPE_THETA = 10000.0
RMS_EPS = 1e-6

LANES = 128
MLA_HEAD_PAD = 128
VT_ROWS = MLA_V + 16
LOG2_E = 1.4426950408889634
VMEM_LIMIT = 56 * 1024 * 1024

TOKEN_TILE = 512
MLP_TILE = 1024
ATTN_BLOCK_Q = 512
ATTN_BLOCK_K = 512
ATTN_UNROLL = 16
GLA_STEP = 256
RET_ROWS = 256
RET_STEP = 512
SSD_STEP = 256
MIXER_STREAMS = 2

_SIZES = (SSD_INNER, SSD_XBC, SSD_HEADS,
          RET_HEADS * RET_QK_DIM, RET_HEADS * RET_QK_DIM, RET_HEADS * RET_V_DIM, RET_HEADS * RET_V_DIM,
          MLA_Q_LORA, MLA_KV_LORA, MLA_ROPE,
          GLA_HEADS * GLA_K_DIM, GLA_HEADS * GLA_K_DIM, GLA_HEADS * GLA_V_DIM, GLA_GATE_RANK,
          GLA_HEADS * GLA_V_DIM, N_BRANCHES * D_MODEL)
_OFFS = tuple(int(v) for v in np.concatenate([[0], np.cumsum(_SIZES)]))
(_O_Z, _O_XBC, _O_DT, _O_RQ, _O_RK, _O_RV, _O_RG, _O_CQ, _O_CKV, _O_KR,
 _O_GQ, _O_GK, _O_GV, _O_GGK, _O_GG, _O_GATE, _O_END) = _OFFS

SMALL_DT_LANE = 0
SMALL_GGK_LANE = 8
MLA_PE_LANE = MLA_NOPE


def _dot(a, b, precision=None):
    return jnp.dot(a, b, preferred_element_type=F32, precision=precision)


def _dot_nt(a, b):
    return lax.dot_general(a, b, (((1,), (1,)), ((), ())), preferred_element_type=F32)


def _dot_tn(a, b):
    return lax.dot_general(a, b, (((0,), (0,)), ((), ())), preferred_element_type=F32)


def _rms(x, g):
    return x * lax.rsqrt(jnp.mean(x * x, axis=-1, keepdims=True) + RMS_EPS) * g


def _silu(x):
    return x * jax.nn.sigmoid(x)


def _softplus(x):
    return jnp.maximum(x, 0.0) + jnp.log1p(jnp.exp(-jnp.abs(x)))


def _params(*sem):
    return pltpu.CompilerParams(dimension_semantics=sem, vmem_limit_bytes=VMEM_LIMIT)


def _const_spec(shape):
    nd = len(shape)
    return pl.BlockSpec(shape, lambda *_: (0,) * nd, pipeline_mode=pl.Buffered(1))


def _tril(n):
    r = lax.broadcasted_iota(jnp.int32, (n, n), 0)
    c = lax.broadcasted_iota(jnp.int32, (n, n), 1)
    return r >= c


ROPE_MLA_SRC = RET_QK_DIM // 2
ROPE_ONE_SRC = ROPE_MLA_SRC + MLA_ROPE // 2


def _rope_kernel(pos_ref, f_ref, p_ref, cr_ref, sr_ref, cm_ref, sm_ref):
    ang = pos_ref[...] * f_ref[...]
    c_hi, c_mid, _ = _split3(jnp.cos(ang))
    s_hi, s_mid, _ = _split3(jnp.sin(ang))
    tabs = _dot(jnp.concatenate([c_hi, c_mid, s_hi, s_mid], axis=1), p_ref[...])
    cr_ref[...] = tabs[:, 0:LANES]
    sr_ref[...] = tabs[:, LANES:2 * LANES]
    cm_ref[...] = tabs[:, 2 * LANES:3 * LANES]
    sm_ref[...] = tabs[:, 3 * LANES:4 * LANES]


def _rope_consts():
    half_r = RET_QK_DIM // 2
    half_m = MLA_ROPE // 2
    inv_r = ROPE_THETA ** (-jnp.arange(half_r, dtype=F32) / half_r)
    inv_m = ROPE_THETA ** (-jnp.arange(half_m, dtype=F32) / half_m)
    freq = jnp.zeros((LANES,), F32).at[0:half_r].set(inv_r).at[ROPE_MLA_SRC:ROPE_MLA_SRC + half_m].set(inv_m)
    pc = np.zeros((LANES, 4 * LANES), np.float32)
    ps = np.zeros((LANES, 4 * LANES), np.float32)
    for l in range(LANES):
        pc[l % half_r, l] = 1.0
        ps[l % half_r, LANES + l] = -1.0 if (l % RET_QK_DIM) < half_r else 1.0
        if MLA_PE_LANE <= l < MLA_PE_LANE + MLA_ROPE:
            i = l - MLA_PE_LANE
            pc[ROPE_MLA_SRC + i % half_m, 2 * LANES + l] = 1.0
            ps[ROPE_MLA_SRC + i % half_m, 3 * LANES + l] = -1.0 if i < half_m else 1.0
        else:
            pc[ROPE_ONE_SRC, 2 * LANES + l] = 1.0
    place = jnp.asarray(np.concatenate([pc, pc, ps, ps], axis=0), BF16)
    return freq.reshape(1, LANES), place


def _rope_tables(positions):
    b, l = positions.shape
    t = b * l
    tb = min(1024, t)
    pos = positions.astype(F32).reshape(t, 1)
    freq, place = _rope_consts()
    tab = jax.ShapeDtypeStruct((t, LANES), F32)
    spec = pl.BlockSpec((tb, LANES), lambda i: (i, 0))
    return pl.pallas_call(
        _rope_kernel,
        out_shape=(tab, tab, tab, tab),
        grid=(t // tb,),
        in_specs=[pl.BlockSpec((tb, 1), lambda i: (i, 0)), _const_spec((1, LANES)), _const_spec(place.shape)],
        out_specs=(spec, spec, spec, spec),
        compiler_params=_params("arbitrary"),
        name="rope_tables",
    )(pos, freq, place)


def _inproj_kernel(x_ref, g_ref, wssd_ref, wret_ref, wmla_ref, wgla_ref, wsm_ref,
                   ossd_ref, oret_ref, omla_ref, ogla_ref, osm_ref):
    h = _rms(x_ref[...], g_ref[...]).astype(BF16)
    ossd_ref[...] = _dot(h, wssd_ref[...]).astype(BF16)
    oret_ref[...] = _dot(h, wret_ref[...]).astype(BF16)
    omla_ref[...] = _dot(h, wmla_ref[...]).astype(BF16)
    ogla_ref[...] = _dot(h, wgla_ref[...]).astype(BF16)
    osm_ref[...] = _dot(h, wsm_ref[...])


def _inproj(x2, g, wssd, wret, wmla, wgla, wsm):
    t = x2.shape[0]
    tm = min(TOKEN_TILE, t)
    ws = (wssd, wret, wmla, wgla, wsm)
    outs = tuple(jax.ShapeDtypeStruct((t, w.shape[1]), BF16) for w in ws[:4]) + (
        jax.ShapeDtypeStruct((t, LANES), F32),)
    return pl.pallas_call(
        _inproj_kernel,
        out_shape=outs,
        grid=(t // tm,),
        in_specs=[pl.BlockSpec((tm, D_MODEL), lambda i: (i, 0)), _const_spec((1, D_MODEL))]
                 + [_const_spec(w.shape) for w in ws],
        out_specs=tuple(pl.BlockSpec((tm, o.shape[1]), lambda i: (i, 0)) for o in outs),
        compiler_params=_params("arbitrary"),
        name="in_proj",
    )(x2, g, *ws)


def _per_stream(body, n_streamed, n_shared, n_out):
    def kernel(*refs):
        streamed = refs[:n_streamed]
        shared = refs[n_streamed:n_streamed + n_shared]
        outs = refs[n_streamed + n_shared:n_streamed + n_shared + n_out]
        scratch = refs[n_streamed + n_shared + n_out:]

        @pl.when(pl.program_id(1) == 0)
        def _():
            for r in scratch:
                r[...] = jnp.zeros(r.shape, r.dtype)

        for s in range(streamed[0].shape[0]):
            body(*(r.at[s] for r in streamed), *shared, *(r.at[s] for r in outs), *(r.at[s] for r in scratch))
    return kernel


def _stream_specs(b, l, rows, widths):
    streams = MIXER_STREAMS if b % MIXER_STREAMS == 0 else 1
    return streams, [pl.BlockSpec((streams, rows, w), lambda bi, ci: (bi, ci, 0)) for w in widths]


def _split3(x):
    hi = x.astype(BF16)
    r = x - hi.astype(F32)
    mid = r.astype(BF16)
    lo = (r - mid.astype(F32)).astype(BF16)
    return hi, mid, lo


def _dot_exact01(a01, x):
    n = x.shape[1]
    parts = _dot(a01, jnp.concatenate(_split3(x), axis=1))
    return parts[:, 0:n] + parts[:, n:2 * n] + parts[:, 2 * n:3 * n]


def _chunk_tril(rows, chunk):
    r = lax.broadcasted_iota(jnp.int32, (rows, rows), 0)
    c = lax.broadcasted_iota(jnp.int32, (rows, rows), 1)
    shift = chunk.bit_length() - 1
    return ((r >> shift) == (c >> shift)) & (r >= c)


def _ssd_body(blk_ref, sm_ref, cw_ref, cb_ref, shift_ref, dtb_ref, alog_ref, exp_ref, dsk_ref, ng_ref,
              y_ref, conv_ref, state_ref):
    q = SSD_CHUNK
    rows = blk_ref.shape[0]
    n_chunks = rows // q
    tail = 8

    z = blk_ref[:, 0:SSD_INNER].astype(F32)
    xbc_b = blk_ref[:, SSD_INNER:SSD_INNER + SSD_XBC]
    xbc = xbc_b.astype(F32)
    conv_ref[tail:2 * tail, :] = xbc[0:tail, :]
    acc = cb_ref[...] + cw_ref[SSD_CONV - 1:SSD_CONV, :] * xbc
    head = acc[0:tail, :]
    for j in range(SSD_CONV - 1):
        off = tail - (SSD_CONV - 1) + j
        acc = acc + cw_ref[j:j + 1, :] * _dot(shift_ref[j], xbc_b)
        head = head + cw_ref[j:j + 1, :] * conv_ref[off:off + tail, :]
    conv_ref[0:tail, :] = xbc[rows - tail:rows, :]
    xbc = _silu(jnp.concatenate([head, acc[tail:, :]], axis=0))
    xs = xbc[:, 0:SSD_INNER]
    gn = SSD_GROUPS * SSD_STATE
    bm = xbc[:, SSD_INNER:SSD_INNER + gn].astype(BF16)
    cm = xbc[:, SSD_INNER + gn:SSD_INNER + 2 * gn].astype(BF16)

    dt = _softplus(sm_ref[...] + dtb_ref[...])
    la = dt * (-jnp.exp(alog_ref[...]))
    tri = jnp.where(_chunk_tril(rows, q), 1.0, 0.0).astype(BF16)
    cum = _dot_exact01(tri, la)
    expand3 = exp_ref[...]
    cum_ch = _dot(jnp.concatenate(_split3(cum), axis=1), expand3)
    dt_ch = _dot(jnp.concatenate(_split3(dt), axis=1), expand3)
    xdt = xs * dt_ch
    xdt_b = xdt.astype(BF16)
    last = jnp.concatenate(
        [jnp.broadcast_to(cum_ch[(j + 1) * q - 1:(j + 1) * q, :], (q, SSD_INNER)) for j in range(n_chunks)], axis=0)
    xdte_b = (xdt * jnp.exp(last - cum_ch)).astype(BF16)

    hpg = SSD_HEADS // SSD_GROUPS
    gw = hpg * SSD_HEAD_DIM
    lane_lo = lax.broadcasted_iota(jnp.int32, (q, LANES), 1) < SSD_HEAD_DIM
    mask = _tril(q)
    state = [state_ref[g] for g in range(SSD_GROUPS)]
    y_rows = []
    for c in range(n_chunks):
        sl = slice(c * q, (c + 1) * q)
        cum_c = cum[sl]
        cum_t = cum_c.T
        chunk_decay = jnp.exp(cum_ch[(c + 1) * q - 1:(c + 1) * q, :])
        ydiag, yoff = [], []
        for g in range(SSD_GROUPS):
            cg = cm[sl, g * SSD_STATE:(g + 1) * SSD_STATE]
            bg = bm[sl, g * SSD_STATE:(g + 1) * SSD_STATE]
            s = _dot_nt(cg, bg)
            yoff.append(_dot(cg, state[g].astype(BF16)))
            for j in range(hpg // 2):
                xpair = xdt_b[sl, g * gw + j * LANES:g * gw + (j + 1) * LANES]
                ys = []
                for e in range(2):
                    h = g * hpg + 2 * j + e
                    seg = cum_c[:, h:h + 1] - cum_t[h:h + 1, :]
                    a = (s * jnp.exp(jnp.where(mask, seg, -jnp.inf))).astype(BF16)
                    ys.append(_dot(a, xpair))
                ydiag.append(jnp.where(lane_lo, ys[0], ys[1]))
            state[g] = (state[g] * chunk_decay[:, g * gw:(g + 1) * gw]
                        + _dot_tn(bg, xdte_b[sl, g * gw:(g + 1) * gw]))
        y_rows.append(jnp.concatenate(ydiag, axis=1) + jnp.concatenate(yoff, axis=1) * jnp.exp(cum_ch[sl]))
    for g in range(SSD_GROUPS):
        state_ref[g] = state[g]
    y = jnp.concatenate(y_rows, axis=0)
    y = (y + dsk_ref[...] * xs) * _silu(z)
    outs = []
    for g in range(SSD_GROUPS):
        outs.append(_rms(y[:, g * gw:(g + 1) * gw], ng_ref[:, g * gw:(g + 1) * gw]))
    y_ref[...] = jnp.concatenate(outs, axis=1).astype(BF16)


def _ssd(o_ssd, o_small, conv_w, conv_b, dt_bias, a_log, d_skip, norm_g, b, l):
    rows = min(SSD_STEP, l)
    nc = l // rows
    pad = LANES - SSD_HEADS
    dtb = jnp.pad(dt_bias, (0, pad)).reshape(1, LANES)
    alog = jnp.pad(a_log, (0, pad)).reshape(1, LANES)
    head_of = np.arange(SSD_INNER) // SSD_HEAD_DIM
    expand = np.arange(LANES)[:, None] == head_of[None, :]
    expand3 = jnp.asarray(np.concatenate([expand] * 3, axis=0), BF16)
    dsk = jnp.repeat(d_skip, SSD_HEAD_DIM).reshape(1, SSD_INNER)
    t = np.arange(rows)
    shifts = jnp.asarray(np.stack([t[None, :] == t[:, None] - (SSD_CONV - 1 - j) for j in range(SSD_CONV - 1)]), BF16)
    streams, (blk_spec, sm_spec, out_spec) = _stream_specs(b, l, rows, (SSD_INNER + SSD_XBC, LANES, SSD_INNER))
    return pl.pallas_call(
        _per_stream(_ssd_body, 2, 8, 1),
        out_shape=jax.ShapeDtypeStruct((b, l, SSD_INNER), BF16),
        grid=(b // streams, nc),
        in_specs=[blk_spec, sm_spec,
                  _const_spec((SSD_CONV, SSD_XBC)), _const_spec((1, SSD_XBC)), _const_spec(shifts.shape),
                  _const_spec((1, LANES)), _const_spec((1, LANES)), _const_spec(expand3.shape),
                  _const_spec((1, SSD_INNER)), _const_spec((1, SSD_INNER))],
        out_specs=out_spec,
        scratch_shapes=[pltpu.VMEM((streams, 16, SSD_XBC), F32),
                        pltpu.VMEM((streams, SSD_GROUPS, SSD_STATE, SSD_INNER // SSD_GROUPS), F32)],
        compiler_params=_params("arbitrary", "arbitrary"),
        name="ssd_mixer",
    )(o_ssd.reshape(b, l, -1), o_small.reshape(b, l, -1), conv_w, conv_b.reshape(1, -1), shifts, dtb, alog, expand3,
      dsk, norm_g.reshape(1, -1)).reshape(b * l, SSD_INNER)


def _head_rows(full, n_heads, rows, cols):
    return jnp.concatenate([full[h * rows:(h + 1) * rows, h * cols:(h + 1) * cols] for h in range(n_heads)], axis=0)


def _ret_body(blk_ref, c_ref, s_ref, dm_ref, qs_ref, ks_ref, cd_ref, ng_ref, y_ref, state_ref):
    w = RET_HEADS * RET_QK_DIM
    vw = RET_HEADS * RET_V_DIM
    rows = dm_ref.shape[1]
    lane = lax.broadcasted_iota(jnp.int32, (1, w), 1)
    first_half = (lane & (RET_QK_DIM - 1)) < RET_QK_DIM // 2
    state = state_ref[...]
    for j in range(blk_ref.shape[0] // rows):
        sl = slice(j * rows, (j + 1) * rows)
        ct = jnp.concatenate([c_ref[sl, :]] * (w // LANES), axis=1)
        st = jnp.concatenate([s_ref[sl, :]] * (w // LANES), axis=1)

        def rope(x):
            partner = jnp.where(first_half, pltpu.roll(x, w - RET_QK_DIM // 2, axis=1),
                                pltpu.roll(x, RET_QK_DIM // 2, axis=1))
            return x * ct + partner * st

        q = rope(blk_ref[sl, 0:w].astype(F32))
        k = rope(blk_ref[sl, w:2 * w].astype(F32)) * (RET_QK_DIM ** -0.5)
        v = blk_ref[sl, 2 * w:2 * w + vw]
        g = blk_ref[sl, 2 * w + vw:2 * w + 2 * vw].astype(F32)
        k_b = k.astype(BF16)
        qo = q * qs_ref[...]
        state_b = state.astype(BF16)
        ys = []
        for h in range(RET_HEADS):
            in_head = (lane >= h * RET_QK_DIM) & (lane < (h + 1) * RET_QK_DIM)
            qh = jnp.where(in_head, q, 0.0).astype(BF16)
            sc = (_dot_nt(qh, k_b) * dm_ref[h]).astype(BF16)
            yh = (_dot(sc, v[:, h * RET_V_DIM:(h + 1) * RET_V_DIM])
                  + _dot(jnp.where(in_head, qo, 0.0).astype(BF16), state_b))
            ys.append(_rms(yh, ng_ref[:, h * RET_V_DIM:(h + 1) * RET_V_DIM]))
        upd = _head_rows(_dot_tn((k * ks_ref[...]).astype(BF16), v), RET_HEADS, RET_QK_DIM, RET_V_DIM)
        state = state * cd_ref[...] + upd
        y_ref[sl, :] = (jnp.concatenate(ys, axis=1) * _silu(g)).astype(BF16)
    state_ref[...] = state


def _ret_consts(rows):
    lg = np.log1p(-np.exp2(-5.0 - np.arange(RET_HEADS, dtype=np.float64)))
    i = np.arange(rows)
    dm = np.where(i[:, None] >= i[None, :], np.exp(lg[:, None, None] * (i[:, None] - i[None, :])[None]), 0.0)
    qs = np.repeat(np.exp(lg[None, :] * (i[:, None] + 1)), RET_QK_DIM, axis=1)
    ks = np.repeat(np.exp(lg[None, :] * (rows - 1 - i[:, None])), RET_QK_DIM, axis=1)
    cd = np.broadcast_to(np.repeat(np.exp(lg * rows), RET_QK_DIM)[:, None], (RET_HEADS * RET_QK_DIM, RET_V_DIM))
    return tuple(jnp.asarray(a, F32) for a in (dm, qs, ks, cd))


def _ret(o_ret, c_tab, s_tab, norm_g, b, l):
    rows = min(RET_STEP, l)
    nc = l // rows
    w = RET_HEADS * RET_QK_DIM
    vw = RET_HEADS * RET_V_DIM
    dm, qs, ks, cd = _ret_consts(min(RET_ROWS, rows))
    streams, (blk_spec, tab_spec, out_spec) = _stream_specs(b, l, rows, (2 * w + 2 * vw, LANES, vw))
    return pl.pallas_call(
        _per_stream(_ret_body, 3, 5, 1),
        out_shape=jax.ShapeDtypeStruct((b, l, vw), BF16),
        grid=(b // streams, nc),
        in_specs=[blk_spec, tab_spec, tab_spec,
                  _const_spec(dm.shape), _const_spec(qs.shape), _const_spec(ks.shape),
                  _const_spec(cd.shape), _const_spec((1, vw))],
        out_specs=out_spec,
        scratch_shapes=[pltpu.VMEM((streams, w, RET_V_DIM), F32)],
        compiler_params=_params("arbitrary", "arbitrary"),
        name="ret_mixer",
    )(o_ret.reshape(b, l, -1), c_tab.reshape(b, l, -1), s_tab.reshape(b, l, -1), dm, qs, ks, cd,
      norm_g.reshape(1, -1)).reshape(b * l, vw)


def _gla_body(blk_ref, sm_ref, w2_ref, bgk_ref, ng_ref, y_ref, state_ref):
    c = GLA_CHUNK
    rows = blk_ref.shape[0]
    kw = GLA_HEADS * GLA_K_DIM
    vw = GLA_HEADS * GLA_V_DIM
    sm_hi, sm_mid, _ = _split3(sm_ref[...])
    logits = _dot(jnp.concatenate([sm_hi, sm_hi, sm_mid], axis=1), w2_ref[...]) + bgk_ref[...]
    log_g = -_softplus(-logits) * (1.0 / GLA_GATE_NORM)
    tri = _chunk_tril(rows, c)
    cum = _dot_exact01(jnp.where(tri, 1.0, 0.0).astype(BF16), log_g)
    n_chunks = rows // c
    total = jnp.concatenate(
        [jnp.broadcast_to(cum[(j + 1) * c - 1:(j + 1) * c, :], (c, kw)) for j in range(n_chunks)], axis=0)
    q = blk_ref[:, 0:kw].astype(F32) * (GLA_K_DIM ** -0.5)
    k = blk_ref[:, kw:2 * kw].astype(F32)
    v = blk_ref[:, 2 * kw:2 * kw + vw]
    g = blk_ref[:, 2 * kw + vw:2 * kw + 2 * vw].astype(F32)
    q_in = q * jnp.exp(cum)
    k_in = (k * jnp.exp(-cum)).astype(BF16)
    k_st = (k * jnp.exp(total - cum)).astype(BF16)
    lane = lax.broadcasted_iota(jnp.int32, (1, kw), 1)
    in_head = [(lane >= h * GLA_K_DIM) & (lane < (h + 1) * GLA_K_DIM) for h in range(GLA_HEADS)]

    state = state_ref[...]
    o_inter = []
    for j in range(n_chunks):
        sl = slice(j * c, (j + 1) * c)
        q_stack = jnp.concatenate([jnp.where(m, q_in[sl], 0.0) for m in in_head], axis=0).astype(BF16)
        k_stack = jnp.concatenate([jnp.where(m, k_st[sl], 0) for m in in_head], axis=0)
        v_stack = jnp.concatenate([v[sl, h * GLA_V_DIM:(h + 1) * GLA_V_DIM] for h in range(GLA_HEADS)], axis=0)
        oi = _dot(q_stack, state.astype(BF16))
        o_inter.append(jnp.concatenate([oi[h * c:(h + 1) * c] for h in range(GLA_HEADS)], axis=1))
        last = cum[(j + 1) * c - 1:(j + 1) * c, :]
        decay = jnp.exp(jnp.broadcast_to(last, (LANES, kw)).T)
        state = state * decay + _dot_tn(k_stack, v_stack)
    state_ref[...] = state
    o_inter = jnp.concatenate(o_inter, axis=0)
    outs = []
    for h in range(GLA_HEADS):
        qh = jnp.where(in_head[h], q_in, 0.0).astype(BF16)
        sc = jnp.where(tri, _dot_nt(qh, k_in), 0.0).astype(BF16)
        oh = _dot(sc, v[:, h * GLA_V_DIM:(h + 1) * GLA_V_DIM]) + o_inter[:, h * GLA_V_DIM:(h + 1) * GLA_V_DIM]
        outs.append(_rms(oh, ng_ref[:, h * GLA_V_DIM:(h + 1) * GLA_V_DIM]))
    y_ref[...] = (jnp.concatenate(outs, axis=1) * _silu(g)).astype(BF16)


def _gla(o_gla, o_small, w_gk2, b_gk, norm_g, b, l):
    rows = min(GLA_STEP, l)
    ns = l // rows
    kw = GLA_HEADS * GLA_K_DIM
    vw = GLA_HEADS * GLA_V_DIM
    w2 = jnp.zeros((LANES, kw), F32).at[SMALL_GGK_LANE:SMALL_GGK_LANE + GLA_GATE_RANK].set(w_gk2)
    w2_hi = w2.astype(BF16)
    w2_mid = (w2 - w2_hi.astype(F32)).astype(BF16)
    w2 = jnp.concatenate([w2_hi, w2_mid, w2_hi], axis=0)
    streams, (blk_spec, sm_spec, out_spec) = _stream_specs(b, l, rows, (2 * kw + 2 * vw, LANES, vw))
    return pl.pallas_call(
        _per_stream(_gla_body, 2, 3, 1),
        out_shape=jax.ShapeDtypeStruct((b, l, vw), BF16),
        grid=(b // streams, ns),
        in_specs=[blk_spec, sm_spec,
                  _const_spec(w2.shape), _const_spec((1, kw)), _const_spec((1, vw))],
        out_specs=out_spec,
        scratch_shapes=[pltpu.VMEM((streams, kw, GLA_V_DIM), F32)],
        compiler_params=_params("arbitrary", "arbitrary"),
        name="gla_mixer",
    )(o_gla.reshape(b, l, -1), o_small.reshape(b, l, -1), w2, b_gk.reshape(1, -1),
      norm_g.reshape(1, -1)).reshape(b * l, vw)


def _mla_proj_kernel(blk_ref, c_ref, s_ref, qn_ref, kvn_ref, wq_ref, wqr_ref, wk_ref, wvt_ref,
                     q_ref, k_ref, vt_ref):
    o_ckv = MLA_Q_LORA
    o_kr = o_ckv + MLA_KV_LORA
    cq = _rms(blk_ref[:, 0:MLA_Q_LORA].astype(F32), qn_ref[...]).astype(BF16)
    ckv = _rms(blk_ref[:, o_ckv:o_kr].astype(F32), kvn_ref[...]).astype(BF16)
    c = c_ref[...]
    s = s_ref[...]
    ct = jnp.concatenate([c] * MLA_HEADS, axis=1)
    st = jnp.concatenate([s] * MLA_HEADS, axis=1)
    scale = (MLA_NOPE + MLA_ROPE) ** -0.5 * LOG2_E
    q = (_dot(cq, wq_ref[...]) * ct + _dot(cq, wqr_ref[...]) * st) * scale
    q_ref[...] = q.astype(BF16)
    kr = blk_ref[:, o_kr:o_kr + LANES].astype(F32)
    lane = lax.broadcasted_iota(jnp.int32, (1, LANES), 1)
    half = MLA_ROPE // 2
    kpe = kr * c + jnp.where(lane < MLA_PE_LANE + half, pltpu.roll(kr, LANES - half, axis=1), pltpu.roll(kr, half, axis=1)) * s
    k = _dot(ckv, wk_ref[...]) + jnp.concatenate([kpe] * MLA_HEADS, axis=1)
    k_ref[...] = k.astype(BF16)
    vt = _dot_nt(wvt_ref[...], ckv).astype(BF16)
    ones = jnp.ones((VT_ROWS - MLA_V, vt.shape[1]), BF16)
    for h in range(MLA_HEADS):
        vt_ref[h, 0:MLA_V, :] = vt[h * MLA_V:(h + 1) * MLA_V, :]
        vt_ref[h, MLA_V:VT_ROWS, :] = ones


def _mla_proj(o_mla, c_tab, s_tab, q_norm, kv_norm, wq, wqr, wk, wvt, b, l):
    tm = min(ATTN_BLOCK_K, l)
    nk = l // tm
    hw = MLA_HEADS * MLA_HEAD_PAD
    row = lambda bi, j: (bi * nk + j, 0)
    return pl.pallas_call(
        _mla_proj_kernel,
        out_shape=(jax.ShapeDtypeStruct((b * l, hw), BF16), jax.ShapeDtypeStruct((b * l, hw), BF16),
                   jax.ShapeDtypeStruct((b, nk, MLA_HEADS, VT_ROWS, tm), BF16)),
        grid=(b, nk),
        in_specs=[pl.BlockSpec((tm, o_mla.shape[1]), row), pl.BlockSpec((tm, LANES), row),
                  pl.BlockSpec((tm, LANES), row),
                  _const_spec((1, MLA_Q_LORA)), _const_spec((1, MLA_KV_LORA)),
                  _const_spec(wq.shape), _const_spec(wqr.shape), _const_spec(wk.shape), _const_spec(wvt.shape)],
        out_specs=(pl.BlockSpec((tm, hw), row), pl.BlockSpec((tm, hw), row),
                   pl.BlockSpec((None, None, MLA_HEADS, VT_ROWS, tm), lambda bi, j: (bi, j, 0, 0, 0))),
        compiler_params=_params("arbitrary", "arbitrary"),
        name="mla_proj",
    )(o_mla, c_tab, s_tab, q_norm.reshape(1, -1), kv_norm.reshape(1, -1), wq, wqr, wk, wvt)


def _flash_kernel(nq, tq_ref, tk_ref, q_ref, k_ref, vt_ref, o_ref, m_ref, acc_ref, s_ref, mb0_ref, mb1_ref):
    bq, bk = ATTN_BLOCK_Q, ATTN_BLOCK_K
    half = bq // 2
    n_off = nq * (nq - 1) // 2
    n_blk = n_off + nq
    m_ref[...] = jnp.full(m_ref.shape, -jnp.inf, F32)
    acc_ref[...] = jnp.zeros(acc_ref.shape, F32)
    mb_refs = (mb0_ref, mb1_ref)

    def produce(t, slot, diagonal):
        q = q_ref[pl.ds(pl.multiple_of(tq_ref[t] * bq, bq), bq), :]
        k = k_ref[pl.ds(pl.multiple_of(tk_ref[t] * bk, bk), bk), :]
        s = _dot_nt(k, q)
        if diagonal:
            kpos = lax.broadcasted_iota(jnp.int32, (bk, bq), 0)
            qpos = lax.broadcasted_iota(jnp.int32, (bk, bq), 1)
            s = jnp.where(kpos <= qpos, s, -jnp.inf)
        s_ref[0:bk, slot * half:(slot + 1) * half] = s[:, 0:half]
        s_ref[bk:2 * bk, slot * half:(slot + 1) * half] = s[:, half:bq]
        mb_refs[slot][...] = jnp.max(s, axis=0, keepdims=True)

    def consume(t, slot, diagonal):
        qi = tq_ref[t]
        m_old = m_ref[qi]
        m_new = jnp.maximum(m_old, mb_refs[slot][...])
        alpha = jnp.exp2(m_old - m_new)
        s = jnp.concatenate([s_ref[0:bk, slot * half:(slot + 1) * half],
                             s_ref[bk:2 * bk, slot * half:(slot + 1) * half]], axis=1)
        p = jnp.exp2(s - m_new).astype(BF16)
        acc = acc_ref[qi] * alpha + _dot(vt_ref[tk_ref[t]], p)
        if diagonal:
            o_ref[qi] = (acc[0:MLA_V, :] / acc[MLA_V:MLA_V + 1, :]).astype(BF16)
        else:
            acc_ref[qi] = acc
            m_ref[qi] = m_new

    def static_steps(t_from, t_to):
        for t in range(t_from, t_to):
            if t + 1 < n_blk:
                produce(t + 1, (t + 1) % 2, t + 1 >= n_off)
            consume(t, t % 2, t >= n_off)

    u = ATTN_UNROLL

    def group_loop(t_from, n_groups, diagonal):
        def group(i, carry):
            t0 = t_from + u * i
            for d in range(u):
                produce(t0 + d + 1, (d + 1) % 2, diagonal)
                consume(t0 + d, d % 2, diagonal)
            return carry

        if n_groups > 0:
            assert t_from % 2 == 0 and u % 2 == 0
            lax.fori_loop(0, n_groups, group, 0)

    produce(0, 0, n_off == 0)
    n_a = max(0, (n_off - 1) // u)
    group_loop(0, n_a, False)
    t_b = min(n_off + n_off % 2, n_blk)
    static_steps(u * n_a, t_b)
    n_b = max(0, (n_blk - 1 - t_b) // u)
    group_loop(t_b, n_b, True)
    static_steps(t_b + u * n_b, n_blk)


def _flash_order(nq):
    off = [(qi, kb) for qi in range(nq) for kb in range(qi)]
    blocks = off + [(qi, qi) for qi in range(nq)]
    return (jnp.asarray([b[0] for b in blocks], jnp.int32), jnp.asarray([b[1] for b in blocks], jnp.int32))


def _flash(q, k, vt, b, l):
    bq, bk = ATTN_BLOCK_Q, ATTN_BLOCK_K
    assert bq == bk
    nq = l // bq
    nk = l // bk
    q3 = q.reshape(b, l, MLA_HEADS * MLA_HEAD_PAD)
    k3 = k.reshape(b, l, MLA_HEADS * MLA_HEAD_PAD)
    tq, tk = _flash_order(nq)
    grid_spec = pltpu.PrefetchScalarGridSpec(
        num_scalar_prefetch=2,
        grid=(b, MLA_HEADS),
        in_specs=[pl.BlockSpec((None, l, MLA_HEAD_PAD), lambda bi, h, *_: (bi, 0, h)),
                  pl.BlockSpec((None, l, MLA_HEAD_PAD), lambda bi, h, *_: (bi, 0, h)),
                  pl.BlockSpec((None, nk, None, VT_ROWS, bk), lambda bi, h, *_: (bi, 0, h, 0, 0))],
        out_specs=pl.BlockSpec((None, None, nq, MLA_V, bq), lambda bi, h, *_: (bi, h, 0, 0, 0)),
        scratch_shapes=[pltpu.VMEM((nq, 1, bq), F32), pltpu.VMEM((nq, VT_ROWS, bq), F32),
                        pltpu.VMEM((2 * bk, bq), F32),
                        pltpu.VMEM((1, bq), F32), pltpu.VMEM((1, bq), F32)])
    return pl.pallas_call(
        functools.partial(_flash_kernel, nq),
        out_shape=jax.ShapeDtypeStruct((b, MLA_HEADS, nq, MLA_V, bq), BF16),
        grid_spec=grid_spec,
        compiler_params=_params("arbitrary", "arbitrary"),
        name="mla_flash",
    )(tq, tk, q3, k3, vt)


def _merge_kernel(x_ref, yssd_ref, yret_ref, ymt_ref, ygla_ref, gpre_ref, wg_ref, bg_ref, wb_ref, wo_ref,
                  gpost_ref, o_ref):
    x = x_ref[...]
    h = _rms(x, gpre_ref[...]).astype(BF16)
    branches = (_dot(yssd_ref[...], wb_ref[0]), _dot(yret_ref[...], wb_ref[1]),
                _dot_tn(ymt_ref[...].reshape(BRANCH_WIDTH, -1), wb_ref[2]), _dot(ygla_ref[...], wb_ref[3]))
    merged = None
    for i, br in enumerate(branches):
        gate = jax.nn.sigmoid(_dot(h, wg_ref[:, i * D_MODEL:(i + 1) * D_MODEL]) + bg_ref[:, i * D_MODEL:(i + 1) * D_MODEL])
        merged = gate * br if merged is None else merged + gate * br
    o = _dot(merged.astype(BF16), wo_ref[...])
    o_ref[...] = x + _rms(o, gpost_ref[...])


def _merge(x3, y_ssd, y_ret, y_mla_t, y_gla, g_pre, wg, b_gate, wb, wo, g_post):
    b, l, _ = x3.shape
    tm = min(TOKEN_TILE, l)
    assert tm == y_mla_t.shape[-1]
    nt = l // tm
    w = BRANCH_WIDTH
    tok = lambda width: pl.BlockSpec((None, tm, width), lambda bi, j: (bi, j, 0))
    return pl.pallas_call(
        _merge_kernel,
        out_shape=jax.ShapeDtypeStruct((b, l, D_MODEL), F32),
        grid=(b, nt),
        in_specs=[tok(D_MODEL), tok(w), tok(w),
                  pl.BlockSpec((None, MLA_HEADS, None, MLA_V, tm), lambda bi, j: (bi, 0, j, 0, 0)), tok(w),
                  _const_spec((1, D_MODEL)), _const_spec(wg.shape), _const_spec((1, N_BRANCHES * D_MODEL)),
                  _const_spec(wb.shape), _const_spec(wo.shape), _const_spec((1, D_MODEL))],
        out_specs=tok(D_MODEL),
        compiler_params=_params("arbitrary", "arbitrary"),
        name="merge_out",
    )(x3, y_ssd.reshape(b, l, w), y_ret.reshape(b, l, w), y_mla_t, y_gla.reshape(b, l, w),
      g_pre, wg, b_gate.reshape(1, -1), wb, wo, g_post)


def _mlp_kernel(x_ref, gpre_ref, wi_ref, wo_ref, gpost_ref, o_ref):
    x = x_ref[...]
    h = _rms(x, gpre_ref[...]).astype(BF16)
    acc = None
    for j in range(D_FF // D_MODEL):
        u = jnp.maximum(_dot(h, wi_ref[:, j * D_MODEL:(j + 1) * D_MODEL]), 0.0)
        part = _dot((u * u).astype(BF16), wo_ref[j * D_MODEL:(j + 1) * D_MODEL, :])
        acc = part if acc is None else acc + part
    o_ref[...] = x + _rms(acc, gpost_ref[...])


def _mlp(x2, g_pre, wi, wo, g_post):
    t = x2.shape[0]
    tm = min(MLP_TILE, t)
    return pl.pallas_call(
        _mlp_kernel,
        out_shape=jax.ShapeDtypeStruct((t, D_MODEL), F32),
        grid=(t // tm,),
        in_specs=[pl.BlockSpec((tm, D_MODEL), lambda i: (i, 0)), _const_spec((1, D_MODEL)),
                  _const_spec(wi.shape), _const_spec(wo.shape), _const_spec((1, D_MODEL))],
        out_specs=pl.BlockSpec((tm, D_MODEL), lambda i: (i, 0)),
        compiler_params=_params("arbitrary"),
        name="mlp",
    )(x2, g_pre, wi, wo, g_post)


def _rot_perm(n_heads, dim):
    half = dim // 2
    return np.concatenate([h * dim + (np.arange(dim) + half) % dim for h in range(n_heads)])


def _layer_weights(w_in, mla_w_uq, mla_w_ukv):
    col = lambda a, b: w_in[:, a:b]
    w_ssd = col(_O_Z, _O_DT).astype(BF16)
    w_ret = col(_O_RQ, _O_CQ).astype(BF16)
    zl = jnp.zeros((D_MODEL, MLA_PE_LANE), F32)
    zr = jnp.zeros((D_MODEL, LANES - MLA_PE_LANE - MLA_ROPE), F32)
    kperm = _rot_perm(1, MLA_ROPE)
    w_mla = jnp.concatenate([col(_O_CQ, _O_KR), zl, col(_O_KR, _O_GQ), zr], axis=1).astype(BF16)
    w_gla = jnp.concatenate([col(_O_GQ, _O_GGK), col(_O_GG, _O_GATE)], axis=1).astype(BF16)
    w_small = jnp.concatenate(
        [col(_O_DT, _O_RQ), col(_O_GGK, _O_GG),
         jnp.zeros((D_MODEL, LANES - SSD_HEADS - GLA_GATE_RANK), F32)], axis=1).astype(BF16)
    w_gate = col(_O_GATE, _O_END).astype(BF16)

    hd = MLA_NOPE + MLA_ROPE
    uq = mla_w_uq.reshape(MLA_Q_LORA, MLA_HEADS, hd)
    padq = jnp.zeros((MLA_Q_LORA, MLA_HEADS, MLA_HEAD_PAD - hd), F32)
    wq = jnp.concatenate([uq, padq], axis=2).reshape(MLA_Q_LORA, -1).astype(BF16)
    uq_rot = uq[:, :, MLA_NOPE + kperm]
    wqr = jnp.concatenate([jnp.zeros((MLA_Q_LORA, MLA_HEADS, MLA_NOPE), F32), uq_rot, padq],
                          axis=2).reshape(MLA_Q_LORA, -1).astype(BF16)
    ukv = mla_w_ukv.reshape(MLA_KV_LORA, MLA_HEADS, MLA_NOPE + MLA_V)
    wk = jnp.concatenate([ukv[:, :, :MLA_NOPE],
                          jnp.zeros((MLA_KV_LORA, MLA_HEADS, MLA_HEAD_PAD - MLA_NOPE), F32)],
                         axis=2).reshape(MLA_KV_LORA, -1).astype(BF16)
    wvt = ukv[:, :, MLA_NOPE:].reshape(MLA_KV_LORA, -1).T.astype(BF16)
    return w_ssd, w_ret, w_mla, w_gla, w_small, w_gate, wq, wqr, wk, wvt


def _layer(x3, tabs, proj, b_gate, ssd_conv_w, ssd_conv_b, ssd_dt_bias, ssd_a_log, ssd_d, ssd_norm, ret_norm,
           mla_q_norm, mla_kv_norm, gla_w_gk2, gla_b_gk, gla_norm, w_branch, w_out,
           norm_pre_mix, norm_post_mix, norm_pre_mlp, norm_post_mlp, w_mlp_in, w_mlp_out):
    b, l, d = x3.shape
    c_ret, s_ret, c_mla, s_mla = tabs
    w_ssd, w_ret, w_mla, w_gla, w_small, w_gate, wq, wqr, wk, wvt = proj
    g_pre = norm_pre_mix.reshape(1, -1)
    x2 = x3.reshape(b * l, d)
    o_ssd, o_ret, o_mla, o_gla, o_small = _inproj(x2, g_pre, w_ssd, w_ret, w_mla, w_gla, w_small)
    y_ssd = _ssd(o_ssd, o_small, ssd_conv_w, ssd_conv_b, ssd_dt_bias, ssd_a_log, ssd_d, ssd_norm, b, l)
    y_ret = _ret(o_ret, c_ret, s_ret, ret_norm, b, l)
    q, k, vt = _mla_proj(o_mla, c_mla, s_mla, mla_q_norm, mla_kv_norm, wq, wqr, wk, wvt, b, l)
    y_mla_t = _flash(q, k, vt, b, l)
    y_gla = _gla(o_gla, o_small, gla_w_gk2, gla_b_gk, gla_norm, b, l)
    x3 = _merge(x3, y_ssd, y_ret, y_mla_t, y_gla, g_pre, w_gate, b_gate, w_branch, w_out,
                norm_post_mix.reshape(1, -1))
    x2 = _mlp(x3.reshape(b * l, d), norm_pre_mlp.reshape(1, -1), w_mlp_in, w_mlp_out, norm_post_mlp.reshape(1, -1))
    return x2.reshape(b, l, d)


def kernel(x, positions, w_in, b_gate, ssd_conv_w, ssd_conv_b, ssd_dt_bias, ssd_a_log, ssd_d, ssd_norm, ret_norm, mla_q_norm, mla_w_uq, mla_kv_norm, mla_w_ukv, gla_w_gk2, gla_b_gk, gla_norm, w_branch, w_out, norm_pre_mix, norm_post_mix, norm_pre_mlp, norm_post_mlp, w_mlp_in, w_mlp_out):
    per_layer = (b_gate, ssd_conv_w, ssd_conv_b, ssd_dt_bias, ssd_a_log, ssd_d, ssd_norm, ret_norm,
                 mla_q_norm, mla_kv_norm, gla_w_gk2, gla_b_gk, gla_norm, w_branch.astype(BF16), w_out.astype(BF16),
                 norm_pre_mix, norm_post_mix, norm_pre_mlp, norm_post_mlp, w_mlp_in.astype(BF16),
                 w_mlp_out.astype(BF16))
    proj = jax.vmap(_layer_weights)(w_in, mla_w_uq, mla_w_ukv)
    tabs = _rope_tables(positions)
    for i in range(w_in.shape[0]):
        x = _layer(x, tabs, tuple(p[i] for p in proj), *(p[i] for p in per_layer))
    return x
```

```python
import functools

import numpy as np
import jax
import jax.numpy as jnp
from jax import lax
from jax.experimental import pallas as pl
from jax.experimental.pallas import tpu as pltpu

F32 = jnp.float32
BF16 = jnp.bfloat16

D_MODEL = 1024
SSD_HEADS = 8
SSD_HEAD_DIM = 64
SSD_INNER = 512
SSD_GROUPS = 2
SSD_STATE = 128
SSD_CONV = 4
SSD_CHUNK = 128
SSD_XBC = 1024
RET_HEADS = 4
RET_QK_DIM = 64
RET_V_DIM = 128
RET_CHUNK = 128
MLA_HEADS = 8
MLA_Q_LORA = 256
MLA_KV_LORA = 128
MLA_NOPE = 64
MLA_ROPE = 32
MLA_V = 64
GLA_HEADS = 4
GLA_K_DIM = 64
GLA_V_DIM = 128
GLA_GATE_RANK = 16
GLA_GATE_NORM = 16.0
GLA_CHUNK = 64
N_BRANCHES = 4
BRANCH_WIDTH = 512
D_FF = 4 * D_MODEL
ROPE_THETA = 10000.0
RMS_EPS = 1e-6

LANES = 128
MLA_HEAD_PAD = 128
VT_ROWS = MLA_V + 16
LOG2_E = 1.4426950408889634
VMEM_LIMIT = 56 * 1024 * 1024

TOKEN_TILE = 512
WIDE_TILE = 1024
ATTN_BLOCK_Q = 512
ATTN_BLOCK_K = 512
ATTN_UNROLL = 16
GLA_STEP = 256
RET_ROWS = 256
RET_STEP = 512
SSD_STEP = 256
MIXER_STREAMS = 4

_SIZES = (SSD_INNER, SSD_XBC, SSD_HEADS,
          RET_HEADS * RET_QK_DIM, RET_HEADS * RET_QK_DIM, RET_HEADS * RET_V_DIM, RET_HEADS * RET_V_DIM,
          MLA_Q_LORA, MLA_KV_LORA, MLA_ROPE,
          GLA_HEADS * GLA_K_DIM, GLA_HEADS * GLA_K_DIM, GLA_HEADS * GLA_V_DIM, GLA_GATE_RANK,
          GLA_HEADS * GLA_V_DIM, N_BRANCHES * D_MODEL)
_OFFS = tuple(int(v) for v in np.concatenate([[0], np.cumsum(_SIZES)]))
(_O_Z, _O_XBC, _O_DT, _O_RQ, _O_RK, _O_RV, _O_RG, _O_CQ, _O_CKV, _O_KR,
 _O_GQ, _O_GK, _O_GV, _O_GGK, _O_GG, _O_GATE, _O_END) = _OFFS

SMALL_DT_LANE = 0
SMALL_GGK_LANE = 8
MLA_PE_LANE = MLA_NOPE


def _dot(a, b, precision=None):
    return jnp.dot(a, b, preferred_element_type=F32, precision=precision)


def _dot_nt(a, b):
    return lax.dot_general(a, b, (((1,), (1,)), ((), ())), preferred_element_type=F32)


def _dot_tn(a, b):
    return lax.dot_general(a, b, (((0,), (0,)), ((), ())), preferred_element_type=F32)


def _rms(x, g):
    return x * lax.rsqrt(jnp.mean(x * x, axis=-1, keepdims=True) + RMS_EPS) * g


def _silu(x):
    return x * jax.nn.sigmoid(x)


def _softplus(x):
    return jnp.maximum(x, 0.0) + jnp.log1p(jnp.exp(-jnp.abs(x)))


def _params(*sem):
    return pltpu.CompilerParams(dimension_semantics=sem, vmem_limit_bytes=VMEM_LIMIT)


def _const_spec(shape):
    nd = len(shape)
    return pl.BlockSpec(shape, lambda *_: (0,) * nd, pipeline_mode=pl.Buffered(1))


def _tril(n):
    r = lax.broadcasted_iota(jnp.int32, (n, n), 0)
    c = lax.broadcasted_iota(jnp.int32, (n, n), 1)
    return r >= c


ROPE_MLA_SRC = RET_QK_DIM // 2
ROPE_ONE_SRC = ROPE_MLA_SRC + MLA_ROPE // 2


def _rope_kernel(pos_ref, f_ref, p_ref, cr_ref, sr_ref, cm_ref, sm_ref):
    ang = pos_ref[...] * f_ref[...]
    c_hi, c_mid, _ = _split3(jnp.cos(ang))
    s_hi, s_mid, _ = _split3(jnp.sin(ang))
    tabs = _dot(jnp.concatenate([c_hi, c_mid, s_hi, s_mid], axis=1), p_ref[...])
    cr_ref[...] = tabs[:, 0:LANES]
    sr_ref[...] = tabs[:, LANES:2 * LANES]
    cm_ref[...] = tabs[:, 2 * LANES:3 * LANES]
    sm_ref[...] = tabs[:, 3 * LANES:4 * LANES]


def _rope_consts():
    half_r = RET_QK_DIM // 2
    half_m = MLA_ROPE // 2
    inv_r = ROPE_THETA ** (-jnp.arange(half_r, dtype=F32) / half_r)
    inv_m = ROPE_THETA ** (-jnp.arange(half_m, dtype=F32) / half_m)
    freq = jnp.zeros((LANES,), F32).at[0:half_r].set(inv_r).at[ROPE_MLA_SRC:ROPE_MLA_SRC + half_m].set(inv_m)
    pc = np.zeros((LANES, 4 * LANES), np.float32)
    ps = np.zeros((LANES, 4 * LANES), np.float32)
    for l in range(LANES):
        pc[l % half_r, l] = 1.0
        ps[l % half_r, LANES + l] = -1.0 if (l % RET_QK_DIM) < half_r else 1.0
        if MLA_PE_LANE <= l < MLA_PE_LANE + MLA_ROPE:
            i = l - MLA_PE_LANE
            pc[ROPE_MLA_SRC + i % half_m, 2 * LANES + l] = 1.0
            ps[ROPE_MLA_SRC + i % half_m, 3 * LANES + l] = -1.0 if i < half_m else 1.0
        else:
            pc[ROPE_ONE_SRC, 2 * LANES + l] = 1.0
    place = jnp.asarray(np.concatenate([pc, pc, ps, ps], axis=0), BF16)
    return freq.reshape(1, LANES), place


def _rope_tables(positions):
    b, l = positions.shape
    t = b * l
    tb = min(1024, t)
    pos = positions.astype(F32).reshape(t, 1)
    freq, place = _rope_consts()
    tab = jax.ShapeDtypeStruct((t, LANES), F32)
    spec = pl.BlockSpec((tb, LANES), lambda i: (i, 0))
    return pl.pallas_call(
        _rope_kernel,
        out_shape=(tab, tab, tab, tab),
        grid=(t // tb,),
        in_specs=[pl.BlockSpec((tb, 1), lambda i: (i, 0)), _const_spec((1, LANES)), _const_spec(place.shape)],
        out_specs=(spec, spec, spec, spec),
        compiler_params=_params("arbitrary"),
        name="rope_tables",
    )(pos, freq, place)


def _inproj_kernel(x_ref, g_ref, wssd_ref, wret_ref, wmla_ref, wgla_ref, wsm_ref,
                   ossd_ref, oret_ref, omla_ref, ogla_ref, osm_ref):
    h = _rms(x_ref[...], g_ref[...]).astype(BF16)
    ossd_ref[...] = _dot(h, wssd_ref[...]).astype(BF16)
    oret_ref[...] = _dot(h, wret_ref[...]).astype(BF16)
    omla_ref[...] = _dot(h, wmla_ref[...]).astype(BF16)
    ogla_ref[...] = _dot(h, wgla_ref[...]).astype(BF16)
    osm_ref[...] = _dot(h, wsm_ref[...])


def _inproj(x2, g, wssd, wret, wmla, wgla, wsm):
    t = x2.shape[0]
    tm = min(WIDE_TILE, t)
    ws = (wssd, wret, wmla, wgla, wsm)
    outs = tuple(jax.ShapeDtypeStruct((t, w.shape[1]), BF16) for w in ws[:4]) + (
        jax.ShapeDtypeStruct((t, LANES), F32),)
    return pl.pallas_call(
        _inproj_kernel,
        out_shape=outs,
        grid=(t // tm,),
        in_specs=[pl.BlockSpec((tm, D_MODEL), lambda i: (i, 0)), _const_spec((1, D_MODEL))]
                 + [_const_spec(w.shape) for w in ws],
        out_specs=tuple(pl.BlockSpec((tm, o.shape[1]), lambda i: (i, 0)) for o in outs),
        compiler_params=_params("arbitrary"),
        name="in_proj",
    )(x2, g, *ws)


def _per_stream(body, n_streamed, n_shared, n_out):
    def kernel(*refs):
        streamed = refs[:n_streamed]
        shared = refs[n_streamed:n_streamed + n_shared]
        outs = refs[n_streamed + n_shared:n_streamed + n_shared + n_out]
        scratch = refs[n_streamed + n_shared + n_out:]

        @pl.when(pl.program_id(1) == 0)
        def _():
            for r in scratch:
                r[...] = jnp.zeros(r.shape, r.dtype)

        for s in range(streamed[0].shape[0]):
            body(*(r.at[s] for r in streamed), *shared, *(r.at[s] for r in outs), *(r.at[s] for r in scratch))
    return kernel


def _stream_specs(b, l, rows, widths):
    streams = MIXER_STREAMS if b % MIXER_STREAMS == 0 else 1
    return streams, [pl.BlockSpec((streams, rows, w), lambda bi, ci: (bi, ci, 0)) for w in widths]


def _split3(x):
    hi = x.astype(BF16)
    r = x - hi.astype(F32)
    mid = r.astype(BF16)
    lo = (r - mid.astype(F32)).astype(BF16)
    return hi, mid, lo


def _dot_exact01(a01, x):
    n = x.shape[1]
    parts = _dot(a01, jnp.concatenate(_split3(x), axis=1))
    return parts[:, 0:n] + parts[:, n:2 * n] + parts[:, 2 * n:3 * n]


def _chunk_tril(rows, chunk):
    r = lax.broadcasted_iota(jnp.int32, (rows, rows), 0)
    c = lax.broadcasted_iota(jnp.int32, (rows, rows), 1)
    shift = chunk.bit_length() - 1
    return ((r >> shift) == (c >> shift)) & (r >= c)


def _ssd_body(blk_ref, sm_ref, cw_ref, cb_ref, shift_ref, dtb_ref, alog_ref, exp_ref, dsk_ref, ng_ref,
              y_ref, conv_ref, state_ref):
    q = SSD_CHUNK
    rows = blk_ref.shape[0]
    n_chunks = rows // q
    tail = 8

    z = blk_ref[:, 0:SSD_INNER].astype(F32)
    xbc_b = blk_ref[:, SSD_INNER:SSD_INNER + SSD_XBC]
    xbc = xbc_b.astype(F32)
    conv_ref[tail:2 * tail, :] = xbc[0:tail, :]
    acc = cb_ref[...] + cw_ref[SSD_CONV - 1:SSD_CONV, :] * xbc
    head = acc[0:tail, :]
    for j in range(SSD_CONV - 1):
        off = tail - (SSD_CONV - 1) + j
        acc = acc + cw_ref[j:j + 1, :] * _dot(shift_ref[j], xbc_b)
        head = head + cw_ref[j:j + 1, :] * conv_ref[off:off + tail, :]
    conv_ref[0:tail, :] = xbc[rows - tail:rows, :]
    xbc = _silu(jnp.concatenate([head, acc[tail:, :]], axis=0))
    xs = xbc[:, 0:SSD_INNER]
    gn = SSD_GROUPS * SSD_STATE
    bm = xbc[:, SSD_INNER:SSD_INNER + gn].astype(BF16)
    cm = xbc[:, SSD_INNER + gn:SSD_INNER + 2 * gn].astype(BF16)

    dt = _softplus(sm_ref[...] + dtb_ref[...])
    la = dt * (-jnp.exp(alog_ref[...]))
    tri = jnp.where(_chunk_tril(rows, q), 1.0, 0.0).astype(BF16)
    cum = _dot_exact01(tri, la)
    expand3 = exp_ref[...]
    cum_ch = _dot(jnp.concatenate(_split3(cum), axis=1), expand3)
    dt_ch = _dot(jnp.concatenate(_split3(dt), axis=1), expand3)
    xdt = xs * dt_ch
    xdt_b = xdt.astype(BF16)
    last = jnp.concatenate(
        [jnp.broadcast_to(cum_ch[(j + 1) * q - 1:(j + 1) * q, :], (q, SSD_INNER)) for j in range(n_chunks)], axis=0)
    xdte_b = (xdt * jnp.exp(last - cum_ch)).astype(BF16)

    hpg = SSD_HEADS // SSD_GROUPS
    gw = hpg * SSD_HEAD_DIM
    lane_lo = lax.broadcasted_iota(jnp.int32, (q, LANES), 1) < SSD_HEAD_DIM
    mask = _tril(q)
    state = [state_ref[g] for g in range(SSD_GROUPS)]
    y_rows = []
    for c in range(n_chunks):
        sl = slice(c * q, (c + 1) * q)
        cum_c = cum[sl]
        cum_t = cum_c.T
        chunk_decay = jnp.exp(cum_ch[(c + 1) * q - 1:(c + 1) * q, :])
        ydiag, yoff = [], []
        for g in range(SSD_GROUPS):
            cg = cm[sl, g * SSD_STATE:(g + 1) * SSD_STATE]
            bg = bm[sl, g * SSD_STATE:(g + 1) * SSD_STATE]
            s = _dot_nt(cg, bg)
            yoff.append(_dot(cg, state[g].astype(BF16)))
            for j in range(hpg // 2):
                xpair = xdt_b[sl, g * gw + j * LANES:g * gw + (j + 1) * LANES]
                ys = []
                for e in range(2):
                    h = g * hpg + 2 * j + e
                    seg = cum_c[:, h:h + 1] - cum_t[h:h + 1, :]
                    a = (s * jnp.exp(jnp.where(mask, seg, -jnp.inf))).astype(BF16)
                    ys.append(_dot(a, xpair))
                ydiag.append(jnp.where(lane_lo, ys[0], ys[1]))
            state[g] = (state[g] * chunk_decay[:, g * gw:(g + 1) * gw]
                        + _dot_tn(bg, xdte_b[sl, g * gw:(g + 1) * gw]))
        y_rows.append(jnp.concatenate(ydiag, axis=1) + jnp.concatenate(yoff, axis=1) * jnp.exp(cum_ch[sl]))
    for g in range(SSD_GROUPS):
        state_ref[g] = state[g]
    y = jnp.concatenate(y_rows, axis=0)
    y = (y + dsk_ref[...] * xs) * _silu(z)
    outs = []
    for g in range(SSD_GROUPS):
        outs.append(_rms(y[:, g * gw:(g + 1) * gw], ng_ref[:, g * gw:(g + 1) * gw]))
    y_ref[...] = jnp.concatenate(outs, axis=1).astype(BF16)


def _ssd(o_ssd, o_small, conv_w, conv_b, dt_bias, a_log, d_skip, norm_g, b, l):
    rows = min(SSD_STEP, l)
    nc = l // rows
    pad = LANES - SSD_HEADS
    dtb = jnp.pad(dt_bias, (0, pad)).reshape(1, LANES)
    alog = jnp.pad(a_log, (0, pad)).reshape(1, LANES)
    head_of = np.arange(SSD_INNER) // SSD_HEAD_DIM
    expand = np.arange(LANES)[:, None] == head_of[None, :]
    expand3 = jnp.asarray(np.concatenate([expand] * 3, axis=0), BF16)
    dsk = jnp.repeat(d_skip, SSD_HEAD_DIM).reshape(1, SSD_INNER)
    t = np.arange(rows)
    shifts = jnp.asarray(np.stack([t[None, :] == t[:, None] - (SSD_CONV - 1 - j) for j in range(SSD_CONV - 1)]), BF16)
    streams, (blk_spec, sm_spec, out_spec) = _stream_specs(b, l, rows, (SSD_INNER + SSD_XBC, LANES, SSD_INNER))
    return pl.pallas_call(
        _per_stream(_ssd_body, 2, 8, 1),
        out_shape=jax.ShapeDtypeStruct((b, l, SSD_INNER), BF16),
        grid=(b // streams, nc),
        in_specs=[blk_spec, sm_spec,
                  _const_spec((SSD_CONV, SSD_XBC)), _const_spec((1, SSD_XBC)), _const_spec(shifts.shape),
                  _const_spec((1, LANES)), _const_spec((1, LANES)), _const_spec(expand3.shape),
                  _const_spec((1, SSD_INNER)), _const_spec((1, SSD_INNER))],
        out_specs=out_spec,
        scratch_shapes=[pltpu.VMEM((streams, 16, SSD_XBC), F32),
                        pltpu.VMEM((streams, SSD_GROUPS, SSD_STATE, SSD_INNER // SSD_GROUPS), F32)],
        compiler_params=_params("arbitrary", "arbitrary"),
        name="ssd_mixer",
    )(o_ssd.reshape(b, l, -1), o_small.reshape(b, l, -1), conv_w, conv_b.reshape(1, -1), shifts, dtb, alog, expand3,
      dsk, norm_g.reshape(1, -1)).reshape(b * l, SSD_INNER)


def _head_rows(full, n_heads, rows, cols):
    return jnp.concatenate([full[h * rows:(h + 1) * rows, h * cols:(h + 1) * cols] for h in range(n_heads)], axis=0)


def _ret_body(blk_ref, c_ref, s_ref, dm_ref, qs_ref, ks_ref, cd_ref, ng_ref, y_ref, state_ref):
    w = RET_HEADS * RET_QK_DIM
    vw = RET_HEADS * RET_V_DIM
    rows = dm_ref.shape[1]
    lane = lax.broadcasted_iota(jnp.int32, (1, w), 1)
    first_half = (lane & (RET_QK_DIM - 1)) < RET_QK_DIM // 2
    state = state_ref[...]
    for j in range(blk_ref.shape[0] // rows):
        sl = slice(j * rows, (j + 1) * rows)
        ct = jnp.concatenate([c_ref[sl, :]] * (w // LANES), axis=1)
        st = jnp.concatenate([s_ref[sl, :]] * (w // LANES), axis=1)

        def rope(x):
            partner = jnp.where(first_half, pltpu.roll(x, w - RET_QK_DIM // 2, axis=1),
                                pltpu.roll(x, RET_QK_DIM // 2, axis=1))
            return x * ct + partner * st

        q = rope(blk_ref[sl, 0:w].astype(F32))
        k = rope(blk_ref[sl, w:2 * w].astype(F32)) * (RET_QK_DIM ** -0.5)
        v = blk_ref[sl, 2 * w:2 * w + vw]
        g = blk_ref[sl, 2 * w + vw:2 * w + 2 * vw].astype(F32)
        k_b = k.astype(BF16)
        qo = q * qs_ref[...]
        state_b = state.astype(BF16)
        ys = []
        for h in range(RET_HEADS):
            in_head = (lane >= h * RET_QK_DIM) & (lane < (h + 1) * RET_QK_DIM)
            qh = jnp.where(in_head, q, 0.0).astype(BF16)
            sc = (_dot_nt(qh, k_b) * dm_ref[h]).astype(BF16)
            yh = (_dot(sc, v[:, h * RET_V_DIM:(h + 1) * RET_V_DIM])
                  + _dot(jnp.where(in_head, qo, 0.0).astype(BF16), state_b))
            ys.append(_rms(yh, ng_ref[:, h * RET_V_DIM:(h + 1) * RET_V_DIM]))
        upd = _head_rows(_dot_tn((k * ks_ref[...]).astype(BF16), v), RET_HEADS, RET_QK_DIM, RET_V_DIM)
        state = state * cd_ref[...] + upd
        y_ref[sl, :] = (jnp.concatenate(ys, axis=1) * _silu(g)).astype(BF16)
    state_ref[...] = state


def _ret_consts(rows):
    lg = np.log1p(-np.exp2(-5.0 - np.arange(RET_HEADS, dtype=np.float64)))
    i = np.arange(rows)
    dm = np.where(i[:, None] >= i[None, :], np.exp(lg[:, None, None] * (i[:, None] - i[None, :])[None]), 0.0)
    qs = np.repeat(np.exp(lg[None, :] * (i[:, None] + 1)), RET_QK_DIM, axis=1)
    ks = np.repeat(np.exp(lg[None, :] * (rows - 1 - i[:, None])), RET_QK_DIM, axis=1)
    cd = np.broadcast_to(np.repeat(np.exp(lg * rows), RET_QK_DIM)[:, None], (RET_HEADS * RET_QK_DIM, RET_V_DIM))
    return tuple(jnp.asarray(a, F32) for a in (dm, qs, ks, cd))


def _ret(o_ret, c_tab, s_tab, norm_g, b, l):
    rows = min(RET_STEP, l)
    nc = l // rows
    w = RET_HEADS * RET_QK_DIM
    vw = RET_HEADS * RET_V_DIM
    dm, qs, ks, cd = _ret_consts(min(RET_ROWS, rows))
    streams, (blk_spec, tab_spec, out_spec) = _stream_specs(b, l, rows, (2 * w + 2 * vw, LANES, vw))
    return pl.pallas_call(
        _per_stream(_ret_body, 3, 5, 1),
        out_shape=jax.ShapeDtypeStruct((b, l, vw), BF16),
        grid=(b // streams, nc),
        in_specs=[blk_spec, tab_spec, tab_spec,
                  _const_spec(dm.shape), _const_spec(qs.shape), _const_spec(ks.shape),
                  _const_spec(cd.shape), _const_spec((1, vw))],
        out_specs=out_spec,
        scratch_shapes=[pltpu.VMEM((streams, w, RET_V_DIM), F32)],
        compiler_params=_params("arbitrary", "arbitrary"),
        name="ret_mixer",
    )(o_ret.reshape(b, l, -1), c_tab.reshape(b, l, -1), s_tab.reshape(b, l, -1), dm, qs, ks, cd,
      norm_g.reshape(1, -1)).reshape(b * l, vw)


def _gla_body(blk_ref, sm_ref, w2_ref, bgk_ref, ng_ref, y_ref, state_ref):
    c = GLA_CHUNK
    rows = blk_ref.shape[0]
    kw = GLA_HEADS * GLA_K_DIM
    vw = GLA_HEADS * GLA_V_DIM
    sm_hi, sm_mid, _ = _split3(sm_ref[...])
    logits = _dot(jnp.concatenate([sm_hi, sm_hi, sm_mid], axis=1), w2_ref[...]) + bgk_ref[...]
    log_g = -_softplus(-logits) * (1.0 / GLA_GATE_NORM)
    tri = _chunk_tril(rows, c)
    cum = _dot_exact01(jnp.where(tri, 1.0, 0.0).astype(BF16), log_g)
    n_chunks = rows // c
    total = jnp.concatenate(
        [jnp.broadcast_to(cum[(j + 1) * c - 1:(j + 1) * c, :], (c, kw)) for j in range(n_chunks)], axis=0)
    q = blk_ref[:, 0:kw].astype(F32) * (GLA_K_DIM ** -0.5)
    k = blk_ref[:, kw:2 * kw].astype(F32)
    v = blk_ref[:, 2 * kw:2 * kw + vw]
    g = blk_ref[:, 2 * kw + vw:2 * kw + 2 * vw].astype(F32)
    q_in = q * jnp.exp(cum)
    k_in = (k * jnp.exp(-cum)).astype(BF16)
    k_st = (k * jnp.exp(total - cum)).astype(BF16)
    lane = lax.broadcasted_iota(jnp.int32, (1, kw), 1)
    in_head = [(lane >= h * GLA_K_DIM) & (lane < (h + 1) * GLA_K_DIM) for h in range(GLA_HEADS)]

    state = state_ref[...]
    o_inter = []
    for j in range(n_chunks):
        sl = slice(j * c, (j + 1) * c)
        q_stack = jnp.concatenate([jnp.where(m, q_in[sl], 0.0) for m in in_head], axis=0).astype(BF16)
        k_stack = jnp.concatenate([jnp.where(m, k_st[sl], 0) for m in in_head], axis=0)
        v_stack = jnp.concatenate([v[sl, h * GLA_V_DIM:(h + 1) * GLA_V_DIM] for h in range(GLA_HEADS)], axis=0)
        oi = _dot(q_stack, state.astype(BF16))
        o_inter.append(jnp.concatenate([oi[h * c:(h + 1) * c] for h in range(GLA_HEADS)], axis=1))
        last = cum[(j + 1) * c - 1:(j + 1) * c, :]
        decay = jnp.exp(jnp.broadcast_to(last, (LANES, kw)).T)
        state = state * decay + _dot_tn(k_stack, v_stack)
    state_ref[...] = state
    o_inter = jnp.concatenate(o_inter, axis=0)
    outs = []
    for h in range(GLA_HEADS):
        qh = jnp.where(in_head[h], q_in, 0.0).astype(BF16)
        sc = jnp.where(tri, _dot_nt(qh, k_in), 0.0).astype(BF16)
        oh = _dot(sc, v[:, h * GLA_V_DIM:(h + 1) * GLA_V_DIM]) + o_inter[:, h * GLA_V_DIM:(h + 1) * GLA_V_DIM]
        outs.append(_rms(oh, ng_ref[:, h * GLA_V_DIM:(h + 1) * GLA_V_DIM]))
    y_ref[...] = (jnp.concatenate(outs, axis=1) * _silu(g)).astype(BF16)


def _gla(o_gla, o_small, w_gk2, b_gk, norm_g, b, l):
    rows = min(GLA_STEP, l)
    ns = l // rows
    kw = GLA_HEADS * GLA_K_DIM
    vw = GLA_HEADS * GLA_V_DIM
    w2 = jnp.zeros((LANES, kw), F32).at[SMALL_GGK_LANE:SMALL_GGK_LANE + GLA_GATE_RANK].set(w_gk2)
    w2_hi = w2.astype(BF16)
    w2_mid = (w2 - w2_hi.astype(F32)).astype(BF16)
    w2 = jnp.concatenate([w2_hi, w2_mid, w2_hi], axis=0)
    streams, (blk_spec, sm_spec, out_spec) = _stream_specs(b, l, rows, (2 * kw + 2 * vw, LANES, vw))
    return pl.pallas_call(
        _per_stream(_gla_body, 2, 3, 1),
        out_shape=jax.ShapeDtypeStruct((b, l, vw), BF16),
        grid=(b // streams, ns),
        in_specs=[blk_spec, sm_spec,
                  _const_spec(w2.shape), _const_spec((1, kw)), _const_spec((1, vw))],
        out_specs=out_spec,
        scratch_shapes=[pltpu.VMEM((streams, kw, GLA_V_DIM), F32)],
        compiler_params=_params("arbitrary", "arbitrary"),
        name="gla_mixer",
    )(o_gla.reshape(b, l, -1), o_small.reshape(b, l, -1), w2, b_gk.reshape(1, -1),
      norm_g.reshape(1, -1)).reshape(b * l, vw)


def _mla_proj_kernel(blk_ref, c_ref, s_ref, qn_ref, kvn_ref, wq_ref, wqr_ref, wk_ref, wvt_ref,
                     q_ref, k_ref, vt_ref):
    o_ckv = MLA_Q_LORA
    o_kr = o_ckv + MLA_KV_LORA
    cq = _rms(blk_ref[:, 0:MLA_Q_LORA].astype(F32), qn_ref[...]).astype(BF16)
    ckv = _rms(blk_ref[:, o_ckv:o_kr].astype(F32), kvn_ref[...]).astype(BF16)
    c = c_ref[...]
    s = s_ref[...]
    ct = jnp.concatenate([c] * MLA_HEADS, axis=1)
    st = jnp.concatenate([s] * MLA_HEADS, axis=1)
    scale = (MLA_NOPE + MLA_ROPE) ** -0.5 * LOG2_E
    q = (_dot(cq, wq_ref[...]) * ct + _dot(cq, wqr_ref[...]) * st) * scale
    q_ref[...] = q.astype(BF16)
    kr = blk_ref[:, o_kr:o_kr + LANES].astype(F32)
    lane = lax.broadcasted_iota(jnp.int32, (1, LANES), 1)
    half = MLA_ROPE // 2
    kpe = kr * c + jnp.where(lane < MLA_PE_LANE + half, pltpu.roll(kr, LANES - half, axis=1), pltpu.roll(kr, half, axis=1)) * s
    k = _dot(ckv, wk_ref[...]) + jnp.concatenate([kpe] * MLA_HEADS, axis=1)
    k_ref[...] = k.astype(BF16)
    vt = _dot_nt(wvt_ref[...], ckv).astype(BF16)
    ones = jnp.ones((VT_ROWS - MLA_V, vt.shape[1]), BF16)
    for h in range(MLA_HEADS):
        vt_ref[h, 0:MLA_V, :] = vt[h * MLA_V:(h + 1) * MLA_V, :]
        vt_ref[h, MLA_V:VT_ROWS, :] = ones


def _mla_proj(o_mla, c_tab, s_tab, q_norm, kv_norm, wq, wqr, wk, wvt, b, l):
    tm = min(ATTN_BLOCK_K, l)
    nk = l // tm
    hw = MLA_HEADS * MLA_HEAD_PAD
    row = lambda bi, j: (bi * nk + j, 0)
    return pl.pallas_call(
        _mla_proj_kernel,
        out_shape=(jax.ShapeDtypeStruct((b * l, hw), BF16), jax.ShapeDtypeStruct((b * l, hw), BF16),
                   jax.ShapeDtypeStruct((b, nk, MLA_HEADS, VT_ROWS, tm), BF16)),
        grid=(b, nk),
        in_specs=[pl.BlockSpec((tm, o_mla.shape[1]), row), pl.BlockSpec((tm, LANES), row),
                  pl.BlockSpec((tm, LANES), row),
                  _const_spec((1, MLA_Q_LORA)), _const_spec((1, MLA_KV_LORA)),
                  _const_spec(wq.shape), _const_spec(wqr.shape), _const_spec(wk.shape), _const_spec(wvt.shape)],
        out_specs=(pl.BlockSpec((tm, hw), row), pl.BlockSpec((tm, hw), row),
                   pl.BlockSpec((None, None, MLA_HEADS, VT_ROWS, tm), lambda bi, j: (bi, j, 0, 0, 0))),
        compiler_params=_params("arbitrary", "arbitrary"),
        name="mla_proj",
    )(o_mla, c_tab, s_tab, q_norm.reshape(1, -1), kv_norm.reshape(1, -1), wq, wqr, wk, wvt)


def _flash_kernel(nq, tq_ref, tk_ref, q_ref, k_ref, vt_ref, o_ref, m_ref, acc_ref, s_ref, mb0_ref, mb1_ref):
    bq, bk = ATTN_BLOCK_Q, ATTN_BLOCK_K
    half = bq // 2
    n_off = nq * (nq - 1) // 2
    n_blk = n_off + nq
    m_ref[...] = jnp.full(m_ref.shape, -jnp.inf, F32)
    acc_ref[...] = jnp.zeros(acc_ref.shape, F32)
    mb_refs = (mb0_ref, mb1_ref)

    hk = bk // 2
    lanes = lambda slot: slice(slot * half, (slot + 1) * half)

    def causal(s):
        kpos = lax.broadcasted_iota(jnp.int32, s.shape, 0)
        qpos = lax.broadcasted_iota(jnp.int32, s.shape, 1)
        return jnp.where(kpos <= qpos, s, -jnp.inf)

    def produce(t, slot, diagonal):
        q = q_ref[pl.ds(pl.multiple_of(tq_ref[t] * bq, bq), bq), :]
        k = k_ref[pl.ds(pl.multiple_of(tk_ref[t] * bk, bk), bk), :]
        if diagonal:
            top = causal(_dot_nt(k[0:hk], q))
            low = causal(_dot_nt(k[hk:bk], q[half:bq]))
            s_ref[0:hk, lanes(slot)] = top[:, 0:half]
            s_ref[bk:bk + hk, lanes(slot)] = top[:, half:bq]
            s_ref[bk + hk:2 * bk, lanes(slot)] = low
            m_top = jnp.max(top, axis=0, keepdims=True)
            mb_refs[slot][...] = jnp.concatenate(
                [m_top[:, 0:half], jnp.maximum(m_top[:, half:bq], jnp.max(low, axis=0, keepdims=True))], axis=1)
        else:
            s = _dot_nt(k, q)
            s_ref[0:bk, lanes(slot)] = s[:, 0:half]
            s_ref[bk:2 * bk, lanes(slot)] = s[:, half:bq]
            mb_refs[slot][...] = jnp.max(s, axis=0, keepdims=True)

    def consume(t, slot, diagonal):
        qi = tq_ref[t]
        m_old = m_ref[qi]
        m_new = jnp.maximum(m_old, mb_refs[slot][...])
        alpha = jnp.exp2(m_old - m_new)
        vt = vt_ref[tk_ref[t]]
        if diagonal:
            p_early = jnp.exp2(s_ref[0:hk, lanes(slot)] - m_new[:, 0:half]).astype(BF16)
            p_late = jnp.exp2(s_ref[bk:2 * bk, lanes(slot)] - m_new[:, half:bq]).astype(BF16)
            pv = jnp.concatenate([_dot(vt[:, 0:hk], p_early), _dot(vt, p_late)], axis=1)
            acc = acc_ref[qi] * alpha + pv
            o_ref[qi] = (acc[0:MLA_V, :] / acc[MLA_V:MLA_V + 1, :]).astype(BF16)
        else:
            s = jnp.concatenate([s_ref[0:bk, lanes(slot)], s_ref[bk:2 * bk, lanes(slot)]], axis=1)
            p = jnp.exp2(s - m_new).astype(BF16)
            acc_ref[qi] = acc_ref[qi] * alpha + _dot(vt, p)
            m_ref[qi] = m_new

    def static_steps(t_from, t_to):
        for t in range(t_from, t_to):
            if t + 1 < n_blk:
                produce(t + 1, (t + 1) % 2, t + 1 >= n_off)
            consume(t, t % 2, t >= n_off)

    u = ATTN_UNROLL

    def group_loop(t_from, n_groups, diagonal):
        def group(i, carry):
            t0 = t_from + u * i
            for d in range(u):
                produce(t0 + d + 1, (d + 1) % 2, diagonal)
                consume(t0 + d, d % 2, diagonal)
            return carry

        if n_groups > 0:
            assert t_from % 2 == 0 and u % 2 == 0
            lax.fori_loop(0, n_groups, group, 0)

    produce(0, 0, n_off == 0)
    n_a = max(0, (n_off - 1) // u)
    group_loop(0, n_a, False)
    t_b = min(n_off + n_off % 2, n_blk)
    static_steps(u * n_a, t_b)
    n_b = max(0, (n_blk - 1 - t_b) // u)
    group_loop(t_b, n_b, True)
    static_steps(t_b + u * n_b, n_blk)


def _flash_order(nq):
    off = [(qi, kb) for qi in range(nq) for kb in range(qi)]
    blocks = off + [(qi, qi) for qi in range(nq)]
    return (jnp.asarray([b[0] for b in blocks], jnp.int32), jnp.asarray([b[1] for b in blocks], jnp.int32))


def _flash(q, k, vt, b, l):
    bq, bk = ATTN_BLOCK_Q, ATTN_BLOCK_K
    assert bq == bk
    nq = l // bq
    nk = l // bk
    q3 = q.reshape(b, l, MLA_HEADS * MLA_HEAD_PAD)
    k3 = k.reshape(b, l, MLA_HEADS * MLA_HEAD_PAD)
    tq, tk = _flash_order(nq)
    grid_spec = pltpu.PrefetchScalarGridSpec(
        num_scalar_prefetch=2,
        grid=(b, MLA_HEADS),
        in_specs=[pl.BlockSpec((None, l, MLA_HEAD_PAD), lambda bi, h, *_: (bi, 0, h)),
                  pl.BlockSpec((None, l, MLA_HEAD_PAD), lambda bi, h, *_: (bi, 0, h)),
                  pl.BlockSpec((None, nk, None, VT_ROWS, bk), lambda bi, h, *_: (bi, 0, h, 0, 0))],
        out_specs=pl.BlockSpec((None, None, nq, MLA_V, bq), lambda bi, h, *_: (bi, h, 0, 0, 0)),
        scratch_shapes=[pltpu.VMEM((nq, 1, bq), F32), pltpu.VMEM((nq, VT_ROWS, bq), F32),
                        pltpu.VMEM((2 * bk, bq), F32),
                        pltpu.VMEM((1, bq), F32), pltpu.VMEM((1, bq), F32)])
    return pl.pallas_call(
        functools.partial(_flash_kernel, nq),
        out_shape=jax.ShapeDtypeStruct((b, MLA_HEADS, nq, MLA_V, bq), BF16),
        grid_spec=grid_spec,
        compiler_params=_params("arbitrary", "arbitrary"),
        name="mla_flash",
    )(tq, tk, q3, k3, vt)


def _merge_kernel(x_ref, yssd_ref, yret_ref, ymt_ref, ygla_ref, gpre_ref, wg_ref, bg_ref, wb_ref, wo_ref,
                  gpost_ref, o_ref):
    x = x_ref[...]
    h = _rms(x, gpre_ref[...]).astype(BF16)
    branches = (_dot(yssd_ref[...], wb_ref[0]), _dot(yret_ref[...], wb_ref[1]),
                _dot_tn(ymt_ref[...].reshape(BRANCH_WIDTH, -1), wb_ref[2]), _dot(ygla_ref[...], wb_ref[3]))
    merged = None
    for i, br in enumerate(branches):
        gate = jax.nn.sigmoid(_dot(h, wg_ref[:, i * D_MODEL:(i + 1) * D_MODEL]) + bg_ref[:, i * D_MODEL:(i + 1) * D_MODEL])
        merged = gate * br if merged is None else merged + gate * br
    o = _dot(merged.astype(BF16), wo_ref[...])
    o_ref[...] = x + _rms(o, gpost_ref[...])


def _merge(x3, y_ssd, y_ret, y_mla_t, y_gla, g_pre, wg, b_gate, wb, wo, g_post):
    b, l, _ = x3.shape
    tm = min(TOKEN_TILE, l)
    assert tm == y_mla_t.shape[-1]
    nt = l // tm
    w = BRANCH_WIDTH
    tok = lambda width: pl.BlockSpec((None, tm, width), lambda bi, j: (bi, j, 0))
    return pl.pallas_call(
        _merge_kernel,
        out_shape=jax.ShapeDtypeStruct((b, l, D_MODEL), F32),
        grid=(b, nt),
        in_specs=[tok(D_MODEL), tok(w), tok(w),
                  pl.BlockSpec((None, MLA_HEADS, None, MLA_V, tm), lambda bi, j: (bi, 0, j, 0, 0)), tok(w),
                  _const_spec((1, D_MODEL)), _const_spec(wg.shape), _const_spec((1, N_BRANCHES * D_MODEL)),
                  _const_spec(wb.shape), _const_spec(wo.shape), _const_spec((1, D_MODEL))],
        out_specs=tok(D_MODEL),
        compiler_params=_params("arbitrary", "arbitrary"),
        name="merge_out",
    )(x3, y_ssd.reshape(b, l, w), y_ret.reshape(b, l, w), y_mla_t, y_gla.reshape(b, l, w),
      g_pre, wg, b_gate.reshape(1, -1), wb, wo, g_post)


def _mlp_kernel(x_ref, gpre_ref, wi_ref, wo_ref, gpost_ref, o_ref):
    x = x_ref[...]
    h = _rms(x, gpre_ref[...]).astype(BF16)
    acc = None
    for j in range(D_FF // D_MODEL):
        u = jnp.maximum(_dot(h, wi_ref[:, j * D_MODEL:(j + 1) * D_MODEL]), 0.0)
        part = _dot((u * u).astype(BF16), wo_ref[j * D_MODEL:(j + 1) * D_MODEL, :])
        acc = part if acc is None else acc + part
    o_ref[...] = x + _rms(acc, gpost_ref[...])


def _mlp(x2, g_pre, wi, wo, g_post):
    t = x2.shape[0]
    tm = min(WIDE_TILE, t)
    return pl.pallas_call(
        _mlp_kernel,
        out_shape=jax.ShapeDtypeStruct((t, D_MODEL), F32),
        grid=(t // tm,),
        in_specs=[pl.BlockSpec((tm, D_MODEL), lambda i: (i, 0)), _const_spec((1, D_MODEL)),
                  _const_spec(wi.shape), _const_spec(wo.shape), _const_spec((1, D_MODEL))],
        out_specs=pl.BlockSpec((tm, D_MODEL), lambda i: (i, 0)),
        compiler_params=_params("arbitrary"),
        name="mlp",
    )(x2, g_pre, wi, wo, g_post)


def _rot_perm(n_heads, dim):
    half = dim // 2
    return np.concatenate([h * dim + (np.arange(dim) + half) % dim for h in range(n_heads)])


def _layer_weights(w_in, mla_w_uq, mla_w_ukv):
    col = lambda a, b: w_in[:, a:b]
    w_ssd = col(_O_Z, _O_DT).astype(BF16)
    w_ret = col(_O_RQ, _O_CQ).astype(BF16)
    zl = jnp.zeros((D_MODEL, MLA_PE_LANE), F32)
    zr = jnp.zeros((D_MODEL, LANES - MLA_PE_LANE - MLA_ROPE), F32)
    kperm = _rot_perm(1, MLA_ROPE)
    w_mla = jnp.concatenate([col(_O_CQ, _O_KR), zl, col(_O_KR, _O_GQ), zr], axis=1).astype(BF16)
    w_gla = jnp.concatenate([col(_O_GQ, _O_GGK), col(_O_GG, _O_GATE)], axis=1).astype(BF16)
    w_small = jnp.concatenate(
        [col(_O_DT, _O_RQ), col(_O_GGK, _O_GG),
         jnp.zeros((D_MODEL, LANES - SSD_HEADS - GLA_GATE_RANK), F32)], axis=1).astype(BF16)
    w_gate = col(_O_GATE, _O_END).astype(BF16)

    hd = MLA_NOPE + MLA_ROPE
    uq = mla_w_uq.reshape(MLA_Q_LORA, MLA_HEADS, hd)
    padq = jnp.zeros((MLA_Q_LORA, MLA_HEADS, MLA_HEAD_PAD - hd), F32)
    wq = jnp.concatenate([uq, padq], axis=2).reshape(MLA_Q_LORA, -1).astype(BF16)
    uq_rot = uq[:, :, MLA_NOPE + kperm]
    wqr = jnp.concatenate([jnp.zeros((MLA_Q_LORA, MLA_HEADS, MLA_NOPE), F32), uq_rot, padq],
                          axis=2).reshape(MLA_Q_LORA, -1).astype(BF16)
    ukv = mla_w_ukv.reshape(MLA_KV_LORA, MLA_HEADS, MLA_NOPE + MLA_V)
    wk = jnp.concatenate([ukv[:, :, :MLA_NOPE],
                          jnp.zeros((MLA_KV_LORA, MLA_HEADS, MLA_HEAD_PAD - MLA_NOPE), F32)],
                         axis=2).reshape(MLA_KV_LORA, -1).astype(BF16)
    wvt = ukv[:, :, MLA_NOPE:].reshape(MLA_KV_LORA, -1).T.astype(BF16)
    return w_ssd, w_ret, w_mla, w_gla, w_small, w_gate, wq, wqr, wk, wvt


def _layer(x3, tabs, proj, b_gate, ssd_conv_w, ssd_conv_b, ssd_dt_bias, ssd_a_log, ssd_d, ssd_norm, ret_norm,
           mla_q_norm, mla_kv_norm, gla_w_gk2, gla_b_gk, gla_norm, w_branch, w_out,
           norm_pre_mix, norm_post_mix, norm_pre_mlp, norm_post_mlp, w_mlp_in, w_mlp_out):
    b, l, d = x3.shape
    c_ret, s_ret, c_mla, s_mla = tabs
    w_ssd, w_ret, w_mla, w_gla, w_small, w_gate, wq, wqr, wk, wvt = proj
    g_pre = norm_pre_mix.reshape(1, -1)
    x2 = x3.reshape(b * l, d)
    o_ssd, o_ret, o_mla, o_gla, o_small = _inproj(x2, g_pre, w_ssd, w_ret, w_mla, w_gla, w_small)
    y_ssd = _ssd(o_ssd, o_small, ssd_conv_w, ssd_conv_b, ssd_dt_bias, ssd_a_log, ssd_d, ssd_norm, b, l)
    y_ret = _ret(o_ret, c_ret, s_ret, ret_norm, b, l)
    q, k, vt = _mla_proj(o_mla, c_mla, s_mla, mla_q_norm, mla_kv_norm, wq, wqr, wk, wvt, b, l)
    y_mla_t = _flash(q, k, vt, b, l)
    y_gla = _gla(o_gla, o_small, gla_w_gk2, gla_b_gk, gla_norm, b, l)
    x3 = _merge(x3, y_ssd, y_ret, y_mla_t, y_gla, g_pre, w_gate, b_gate, w_branch, w_out,
                norm_post_mix.reshape(1, -1))
    x2 = _mlp(x3.reshape(b * l, d), norm_pre_mlp.reshape(1, -1), w_mlp_in, w_mlp_out, norm_post_mlp.reshape(1, -1))
    return x2.reshape(b, l, d)


def kernel(x, positions, w_in, b_gate, ssd_conv_w, ssd_conv_b, ssd_dt_bias, ssd_a_log, ssd_d, ssd_norm, ret_norm, mla_q_norm, mla_w_uq, mla_kv_norm, mla_w_ukv, gla_w_gk2, gla_b_gk, gla_norm, w_branch, w_out, norm_pre_mix, norm_post_mix, norm_pre_mlp, norm_post_mlp, w_mlp_in, w_mlp_out):
    per_layer = (b_gate, ssd_conv_w, ssd_conv_b, ssd_dt_bias, ssd_a_log, ssd_d, ssd_norm, ret_norm,
                 mla_q_norm, mla_kv_norm, gla_w_gk2, gla_b_gk, gla_norm, w_branch.astype(BF16), w_out.astype(BF16),
                 norm_pre_mix, norm_post_mix, norm_pre_mlp, norm_post_mlp, w_mlp_in.astype(BF16),
                 w_mlp_out.astype(BF16))
    proj = jax.vmap(_layer_weights)(w_in, mla_w_uq, mla_w_ukv)
    tabs = _rope_tables(positions)
    for i in range(w_in.shape[0]):
        x = _layer(x, tabs, tuple(p[i] for p in proj), *(p[i] for p in per_layer))
    return x
```

```python
import functools

import numpy as np
import jax
import jax.numpy as jnp
from jax import lax
from jax.experimental import pallas as pl
from jax.experimental.pallas import tpu as pltpu

F32 = jnp.float32
BF16 = jnp.bfloat16

D_MODEL = 1024
SSD_HEADS = 8
SSD_HEAD_DIM = 64
SSD_INNER = 512
SSD_GROUPS = 2
SSD_STATE = 128
SSD_CONV = 4
SSD_CHUNK = 128
SSD_XBC = 1024
RET_HEADS = 4
RET_QK_DIM = 64
RET_V_DIM = 128
MLA_HEADS = 8
MLA_Q_LORA = 256
MLA_KV_LORA = 128
MLA_NOPE = 64
MLA_ROPE = 32
MLA_V = 64
GLA_HEADS = 4
GLA_K_DIM = 64
GLA_V_DIM = 128
GLA_GATE_RANK = 16
GLA_GATE_NORM = 16.0
GLA_CHUNK = 64
N_BRANCHES = 4
BRANCH_WIDTH = 512
D_FF = 4 * D_MODEL
ROPE_THETA = 10000.0
RMS_EPS = 1e-6

LANES = 128
MLA_HEAD_PAD = 128
VT_ROWS = MLA_V + 16
LOG2_E = 1.4426950408889634
VMEM_LIMIT = 56 * 1024 * 1024

TOKEN_TILE = 512
WIDE_TILE = 1024
ATTN_BLOCK_Q = 512
ATTN_BLOCK_K = 512
ATTN_UNROLL = 16
MLA_PROJ_BLOCKS = 2
GLA_STEP = 256
RET_ROWS = 256
RET_STEP = 512
SSD_STEP = 256
MIXER_STREAMS = 4

_SIZES = (SSD_INNER, SSD_XBC, SSD_HEADS,
          RET_HEADS * RET_QK_DIM, RET_HEADS * RET_QK_DIM, RET_HEADS * RET_V_DIM, RET_HEADS * RET_V_DIM,
          MLA_Q_LORA, MLA_KV_LORA, MLA_ROPE,
          GLA_HEADS * GLA_K_DIM, GLA_HEADS * GLA_K_DIM, GLA_HEADS * GLA_V_DIM, GLA_GATE_RANK,
          GLA_HEADS * GLA_V_DIM, N_BRANCHES * D_MODEL)
_OFFS = tuple(int(v) for v in np.concatenate([[0], np.cumsum(_SIZES)]))
(_O_Z, _O_XBC, _O_DT, _O_RQ, _O_RK, _O_RV, _O_RG, _O_CQ, _O_CKV, _O_KR,
 _O_GQ, _O_GK, _O_GV, _O_GGK, _O_GG, _O_GATE, _O_END) = _OFFS

SMALL_GGK_LANE = 8
MLA_PE_LANE = MLA_NOPE


def _dot(a, b, precision=None):
    return jnp.dot(a, b, preferred_element_type=F32, precision=precision)


def _dot_nt(a, b):
    return lax.dot_general(a, b, (((1,), (1,)), ((), ())), preferred_element_type=F32)


def _dot_tn(a, b):
    return lax.dot_general(a, b, (((0,), (0,)), ((), ())), preferred_element_type=F32)


def _rms(x, g):
    return x * lax.rsqrt(jnp.mean(x * x, axis=-1, keepdims=True) + RMS_EPS) * g


def _silu(x):
    return x * jax.nn.sigmoid(x)


def _softplus(x):
    return jnp.maximum(x, 0.0) + jnp.log1p(jnp.exp(-jnp.abs(x)))


def _params(*sem):
    return pltpu.CompilerParams(dimension_semantics=sem, vmem_limit_bytes=VMEM_LIMIT)


def _const_spec(shape):
    nd = len(shape)
    return pl.BlockSpec(shape, lambda *_: (0,) * nd, pipeline_mode=pl.Buffered(1))


def _tril(n):
    r = lax.broadcasted_iota(jnp.int32, (n, n), 0)
    c = lax.broadcasted_iota(jnp.int32, (n, n), 1)
    return r >= c


ROPE_MLA_SRC = RET_QK_DIM // 2
ROPE_ONE_SRC = ROPE_MLA_SRC + MLA_ROPE // 2


def _rope_kernel(pos_ref, f_ref, p_ref, cr_ref, sr_ref, cm_ref, sm_ref):
    ang = pos_ref[...] * f_ref[...]
    c_hi, c_mid, _ = _split3(jnp.cos(ang))
    s_hi, s_mid, _ = _split3(jnp.sin(ang))
    tabs = _dot(jnp.concatenate([c_hi, c_mid, s_hi, s_mid], axis=1), p_ref[...])
    cr_ref[...] = tabs[:, 0:LANES]
    sr_ref[...] = tabs[:, LANES:2 * LANES]
    cm_ref[...] = tabs[:, 2 * LANES:3 * LANES]
    sm_ref[...] = tabs[:, 3 * LANES:4 * LANES]


def _rope_consts():
    half_r = RET_QK_DIM // 2
    half_m = MLA_ROPE // 2
    inv_r = ROPE_THETA ** (-jnp.arange(half_r, dtype=F32) / half_r)
    inv_m = ROPE_THETA ** (-jnp.arange(half_m, dtype=F32) / half_m)
    freq = jnp.zeros((LANES,), F32).at[0:half_r].set(inv_r).at[ROPE_MLA_SRC:ROPE_MLA_SRC + half_m].set(inv_m)
    pc = np.zeros((LANES, 4 * LANES), np.float32)
    ps = np.zeros((LANES, 4 * LANES), np.float32)
    for l in range(LANES):
        pc[l % half_r, l] = 1.0
        ps[l % half_r, LANES + l] = -1.0 if (l % RET_QK_DIM) < half_r else 1.0
        if MLA_PE_LANE <= l < MLA_PE_LANE + MLA_ROPE:
            i = l - MLA_PE_LANE
            pc[ROPE_MLA_SRC + i % half_m, 2 * LANES + l] = 1.0
            ps[ROPE_MLA_SRC + i % half_m, 3 * LANES + l] = -1.0 if i < half_m else 1.0
        else:
            pc[ROPE_ONE_SRC, 2 * LANES + l] = 1.0
    place = jnp.asarray(np.concatenate([pc, pc, ps, ps], axis=0), BF16)
    return freq.reshape(1, LANES), place


def _rope_tables(positions):
    b, l = positions.shape
    t = b * l
    tb = min(1024, t)
    pos = positions.astype(F32).reshape(t, 1)
    freq, place = _rope_consts()
    tab = jax.ShapeDtypeStruct((t, LANES), F32)
    spec = pl.BlockSpec((tb, LANES), lambda i: (i, 0))
    return pl.pallas_call(
        _rope_kernel,
        out_shape=(tab, tab, tab, tab),
        grid=(t // tb,),
        in_specs=[pl.BlockSpec((tb, 1), lambda i: (i, 0)), _const_spec((1, LANES)), _const_spec(place.shape)],
        out_specs=(spec, spec, spec, spec),
        compiler_params=_params("arbitrary"),
        name="rope_tables",
    )(pos, freq, place)


def _inproj_kernel(x_ref, g_ref, wssd_ref, wret_ref, wmla_ref, wgla_ref, wsm_ref,
                   ossd_ref, oret_ref, omla_ref, ogla_ref, osm_ref):
    h = _rms(x_ref[...], g_ref[...]).astype(BF16)
    ossd_ref[...] = _dot(h, wssd_ref[...]).astype(BF16)
    oret_ref[...] = _dot(h, wret_ref[...]).astype(BF16)
    omla_ref[...] = _dot(h, wmla_ref[...]).astype(BF16)
    ogla_ref[...] = _dot(h, wgla_ref[...]).astype(BF16)
    osm_ref[...] = _dot(h, wsm_ref[...])


def _inproj(x2, g, wssd, wret, wmla, wgla, wsm):
    t = x2.shape[0]
    tm = min(WIDE_TILE, t)
    ws = (wssd, wret, wmla, wgla, wsm)
    outs = tuple(jax.ShapeDtypeStruct((t, w.shape[1]), BF16) for w in ws[:4]) + (
        jax.ShapeDtypeStruct((t, LANES), F32),)
    return pl.pallas_call(
        _inproj_kernel,
        out_shape=outs,
        grid=(t // tm,),
        in_specs=[pl.BlockSpec((tm, D_MODEL), lambda i: (i, 0)), _const_spec((1, D_MODEL))]
                 + [_const_spec(w.shape) for w in ws],
        out_specs=tuple(pl.BlockSpec((tm, o.shape[1]), lambda i: (i, 0)) for o in outs),
        compiler_params=_params("arbitrary"),
        name="in_proj",
    )(x2, g, *ws)


def _per_stream(body, n_streamed, n_shared, n_out):
    def kernel(*refs):
        streamed = refs[:n_streamed]
        shared = refs[n_streamed:n_streamed + n_shared]
        outs = refs[n_streamed + n_shared:n_streamed + n_shared + n_out]
        scratch = refs[n_streamed + n_shared + n_out:]

        @pl.when(pl.program_id(1) == 0)
        def _():
            for r in scratch:
                r[...] = jnp.zeros(r.shape, r.dtype)

        for s in range(streamed[0].shape[0]):
            body(*(r.at[s] for r in streamed), *shared, *(r.at[s] for r in outs), *(r.at[s] for r in scratch))
    return kernel


def _stream_specs(b, l, rows, widths):
    streams = MIXER_STREAMS if b % MIXER_STREAMS == 0 else 1
    return streams, [pl.BlockSpec((streams, rows, w), lambda bi, ci: (bi, ci, 0)) for w in widths]


def _split3(x):
    hi = x.astype(BF16)
    r = x - hi.astype(F32)
    mid = r.astype(BF16)
    lo = (r - mid.astype(F32)).astype(BF16)
    return hi, mid, lo


def _dot_exact01(a01, x):
    n = x.shape[1]
    parts = _dot(a01, jnp.concatenate(_split3(x), axis=1))
    return parts[:, 0:n] + parts[:, n:2 * n] + parts[:, 2 * n:3 * n]


def _chunk_tril(rows, chunk):
    r = lax.broadcasted_iota(jnp.int32, (rows, rows), 0)
    c = lax.broadcasted_iota(jnp.int32, (rows, rows), 1)
    shift = chunk.bit_length() - 1
    return ((r >> shift) == (c >> shift)) & (r >= c)


def _ssd_body(blk_ref, sm_ref, cw_ref, cb_ref, shift_ref, dtb_ref, alog_ref, exp_ref, dsk_ref, ng_ref,
              y_ref, conv_ref, state_ref):
    q = SSD_CHUNK
    rows = blk_ref.shape[0]
    n_chunks = rows // q
    tail = 8

    z = blk_ref[:, 0:SSD_INNER].astype(F32)
    xbc_b = blk_ref[:, SSD_INNER:SSD_INNER + SSD_XBC]
    xbc = xbc_b.astype(F32)
    conv_ref[tail:2 * tail, :] = xbc[0:tail, :]
    acc = cb_ref[...] + cw_ref[SSD_CONV - 1:SSD_CONV, :] * xbc
    head = acc[0:tail, :]
    for j in range(SSD_CONV - 1):
        off = tail - (SSD_CONV - 1) + j
        acc = acc + cw_ref[j:j + 1, :] * _dot(shift_ref[j], xbc_b)
        head = head + cw_ref[j:j + 1, :] * conv_ref[off:off + tail, :]
    conv_ref[0:tail, :] = xbc[rows - tail:rows, :]
    xbc = _silu(jnp.concatenate([head, acc[tail:, :]], axis=0))
    xs = xbc[:, 0:SSD_INNER]
    gn = SSD_GROUPS * SSD_STATE
    bm = xbc[:, SSD_INNER:SSD_INNER + gn].astype(BF16)
    cm = xbc[:, SSD_INNER + gn:SSD_INNER + 2 * gn].astype(BF16)

    dt = _softplus(sm_ref[...] + dtb_ref[...])
    la = dt * (-jnp.exp(alog_ref[...]))
    tri = jnp.where(_chunk_tril(rows, q), 1.0, 0.0).astype(BF16)
    cum = _dot_exact01(tri, la)
    expand3 = exp_ref[...]
    cum_ch = _dot(jnp.concatenate(_split3(cum), axis=1), expand3)
    dt_ch = _dot(jnp.concatenate(_split3(dt), axis=1), expand3)
    xdt = xs * dt_ch
    xdt_b = xdt.astype(BF16)
    last = jnp.concatenate(
        [jnp.broadcast_to(cum_ch[(j + 1) * q - 1:(j + 1) * q, :], (q, SSD_INNER)) for j in range(n_chunks)], axis=0)
    xdte_b = (xdt * jnp.exp(last - cum_ch)).astype(BF16)

    hpg = SSD_HEADS // SSD_GROUPS
    gw = hpg * SSD_HEAD_DIM
    lane_lo = lax.broadcasted_iota(jnp.int32, (q, LANES), 1) < SSD_HEAD_DIM
    mask = _tril(q)
    state = [state_ref[g] for g in range(SSD_GROUPS)]
    y_rows = []
    for c in range(n_chunks):
        sl = slice(c * q, (c + 1) * q)
        cum_c = cum[sl]
        cum_t = cum_c.T
        chunk_decay = jnp.exp(cum_ch[(c + 1) * q - 1:(c + 1) * q, :])
        ydiag, yoff = [], []
        for g in range(SSD_GROUPS):
            cg = cm[sl, g * SSD_STATE:(g + 1) * SSD_STATE]
            bg = bm[sl, g * SSD_STATE:(g + 1) * SSD_STATE]
            s = _dot_nt(cg, bg)
            yoff.append(_dot(cg, state[g].astype(BF16)))
            for j in range(hpg // 2):
                xpair = xdt_b[sl, g * gw + j * LANES:g * gw + (j + 1) * LANES]
                ys = []
                for e in range(2):
                    h = g * hpg + 2 * j + e
                    seg = cum_c[:, h:h + 1] - cum_t[h:h + 1, :]
                    a = (s * jnp.exp(jnp.where(mask, seg, -jnp.inf))).astype(BF16)
                    ys.append(_dot(a, xpair))
                ydiag.append(jnp.where(lane_lo, ys[0], ys[1]))
            state[g] = (state[g] * chunk_decay[:, g * gw:(g + 1) * gw]
                        + _dot_tn(bg, xdte_b[sl, g * gw:(g + 1) * gw]))
        y_rows.append(jnp.concatenate(ydiag, axis=1) + jnp.concatenate(yoff, axis=1) * jnp.exp(cum_ch[sl]))
    for g in range(SSD_GROUPS):
        state_ref[g] = state[g]
    y = jnp.concatenate(y_rows, axis=0)
    y = (y + dsk_ref[...] * xs) * _silu(z)
    outs = []
    for g in range(SSD_GROUPS):
        outs.append(_rms(y[:, g * gw:(g + 1) * gw], ng_ref[:, g * gw:(g + 1) * gw]))
    y_ref[...] = jnp.concatenate(outs, axis=1).astype(BF16)


def _ssd(o_ssd, o_small, conv_w, conv_b, dt_bias, a_log, d_skip, norm_g, b, l):
    rows = min(SSD_STEP, l)
    nc = l // rows
    pad = LANES - SSD_HEADS
    dtb = jnp.pad(dt_bias, (0, pad)).reshape(1, LANES)
    alog = jnp.pad(a_log, (0, pad)).reshape(1, LANES)
    head_of = np.arange(SSD_INNER) // SSD_HEAD_DIM
    expand = np.arange(LANES)[:, None] == head_of[None, :]
    expand3 = jnp.asarray(np.concatenate([expand] * 3, axis=0), BF16)
    dsk = jnp.repeat(d_skip, SSD_HEAD_DIM).reshape(1, SSD_INNER)
    t = np.arange(rows)
    shifts = jnp.asarray(np.stack([t[None, :] == t[:, None] - (SSD_CONV - 1 - j) for j in range(SSD_CONV - 1)]), BF16)
    streams, (blk_spec, sm_spec, out_spec) = _stream_specs(b, l, rows, (SSD_INNER + SSD_XBC, LANES, SSD_INNER))
    return pl.pallas_call(
        _per_stream(_ssd_body, 2, 8, 1),
        out_shape=jax.ShapeDtypeStruct((b, l, SSD_INNER), BF16),
        grid=(b // streams, nc),
        in_specs=[blk_spec, sm_spec,
                  _const_spec((SSD_CONV, SSD_XBC)), _const_spec((1, SSD_XBC)), _const_spec(shifts.shape),
                  _const_spec((1, LANES)), _const_spec((1, LANES)), _const_spec(expand3.shape),
                  _const_spec((1, SSD_INNER)), _const_spec((1, SSD_INNER))],
        out_specs=out_spec,
        scratch_shapes=[pltpu.VMEM((streams, 16, SSD_XBC), F32),
                        pltpu.VMEM((streams, SSD_GROUPS, SSD_STATE, SSD_INNER // SSD_GROUPS), F32)],
        compiler_params=_params("arbitrary", "arbitrary"),
        name="ssd_mixer",
    )(o_ssd.reshape(b, l, -1), o_small.reshape(b, l, -1), conv_w, conv_b.reshape(1, -1), shifts, dtb, alog, expand3,
      dsk, norm_g.reshape(1, -1)).reshape(b * l, SSD_INNER)


def _head_rows(full, n_heads, rows, cols):
    return jnp.concatenate([full[h * rows:(h + 1) * rows, h * cols:(h + 1) * cols] for h in range(n_heads)], axis=0)


def _ret_body(blk_ref, c_ref, s_ref, dm_ref, qs_ref, ks_ref, cd_ref, ng_ref, y_ref, state_ref):
    w = RET_HEADS * RET_QK_DIM
    vw = RET_HEADS * RET_V_DIM
    rows = dm_ref.shape[1]
    lane = lax.broadcasted_iota(jnp.int32, (1, w), 1)
    first_half = (lane & (RET_QK_DIM - 1)) < RET_QK_DIM // 2
    state = state_ref[...]
    for j in range(blk_ref.shape[0] // rows):
        sl = slice(j * rows, (j + 1) * rows)
        ct = jnp.concatenate([c_ref[sl, :]] * (w // LANES), axis=1)
        st = jnp.concatenate([s_ref[sl, :]] * (w // LANES), axis=1)

        def rope(x):
            partner = jnp.where(first_half, pltpu.roll(x, w - RET_QK_DIM // 2, axis=1),
                                pltpu.roll(x, RET_QK_DIM // 2, axis=1))
            return x * ct + partner * st

        q = rope(blk_ref[sl, 0:w].astype(F32))
        k = rope(blk_ref[sl, w:2 * w].astype(F32)) * (RET_QK_DIM ** -0.5)
        v = blk_ref[sl, 2 * w:2 * w + vw]
        g = blk_ref[sl, 2 * w + vw:2 * w + 2 * vw].astype(F32)
        k_b = k.astype(BF16)
        qo = q * qs_ref[...]
        state_b = state.astype(BF16)
        ys = []
        for h in range(RET_HEADS):
            in_head = (lane >= h * RET_QK_DIM) & (lane < (h + 1) * RET_QK_DIM)
            qh = jnp.where(in_head, q, 0.0).astype(BF16)
            sc = (_dot_nt(qh, k_b) * dm_ref[h]).astype(BF16)
            yh = (_dot(sc, v[:, h * RET_V_DIM:(h + 1) * RET_V_DIM])
                  + _dot(jnp.where(in_head, qo, 0.0).astype(BF16), state_b))
            ys.append(_rms(yh, ng_ref[:, h * RET_V_DIM:(h + 1) * RET_V_DIM]))
        upd = _head_rows(_dot_tn((k * ks_ref[...]).astype(BF16), v), RET_HEADS, RET_QK_DIM, RET_V_DIM)
        state = state * cd_ref[...] + upd
        y_ref[sl, :] = (jnp.concatenate(ys, axis=1) * _silu(g)).astype(BF16)
    state_ref[...] = state


def _ret_consts(rows):
    lg = np.log1p(-np.exp2(-5.0 - np.arange(RET_HEADS, dtype=np.float64)))
    i = np.arange(rows)
    dm = np.where(i[:, None] >= i[None, :], np.exp(lg[:, None, None] * (i[:, None] - i[None, :])[None]), 0.0)
    qs = np.repeat(np.exp(lg[None, :] * (i[:, None] + 1)), RET_QK_DIM, axis=1)
    ks = np.repeat(np.exp(lg[None, :] * (rows - 1 - i[:, None])), RET_QK_DIM, axis=1)
    cd = np.broadcast_to(np.repeat(np.exp(lg * rows), RET_QK_DIM)[:, None], (RET_HEADS * RET_QK_DIM, RET_V_DIM))
    return tuple(jnp.asarray(a, F32) for a in (dm, qs, ks, cd))


def _ret(o_ret, c_tab, s_tab, norm_g, b, l):
    rows = min(RET_STEP, l)
    nc = l // rows
    w = RET_HEADS * RET_QK_DIM
    vw = RET_HEADS * RET_V_DIM
    dm, qs, ks, cd = _ret_consts(min(RET_ROWS, rows))
    streams, (blk_spec, tab_spec, out_spec) = _stream_specs(b, l, rows, (2 * w + 2 * vw, LANES, vw))
    return pl.pallas_call(
        _per_stream(_ret_body, 3, 5, 1),
        out_shape=jax.ShapeDtypeStruct((b, l, vw), BF16),
        grid=(b // streams, nc),
        in_specs=[blk_spec, tab_spec, tab_spec,
                  _const_spec(dm.shape), _const_spec(qs.shape), _const_spec(ks.shape),
                  _const_spec(cd.shape), _const_spec((1, vw))],
        out_specs=out_spec,
        scratch_shapes=[pltpu.VMEM((streams, w, RET_V_DIM), F32)],
        compiler_params=_params("arbitrary", "arbitrary"),
        name="ret_mixer",
    )(o_ret.reshape(b, l, -1), c_tab.reshape(b, l, -1), s_tab.reshape(b, l, -1), dm, qs, ks, cd,
      norm_g.reshape(1, -1)).reshape(b * l, vw)


def _gla_body(blk_ref, sm_ref, w2_ref, bgk_ref, ng_ref, y_ref, state_ref):
    c = GLA_CHUNK
    rows = blk_ref.shape[0]
    kw = GLA_HEADS * GLA_K_DIM
    vw = GLA_HEADS * GLA_V_DIM
    sm_hi, sm_mid, _ = _split3(sm_ref[...])
    logits = _dot(jnp.concatenate([sm_hi, sm_hi, sm_mid], axis=1), w2_ref[...]) + bgk_ref[...]
    log_g = -_softplus(-logits) * (1.0 / GLA_GATE_NORM)
    tri = _chunk_tril(rows, c)
    cum = _dot_exact01(jnp.where(tri, 1.0, 0.0).astype(BF16), log_g)
    n_chunks = rows // c
    total = jnp.concatenate(
        [jnp.broadcast_to(cum[(j + 1) * c - 1:(j + 1) * c, :], (c, kw)) for j in range(n_chunks)], axis=0)
    q = blk_ref[:, 0:kw].astype(F32) * (GLA_K_DIM ** -0.5)
    k = blk_ref[:, kw:2 * kw].astype(F32)
    v = blk_ref[:, 2 * kw:2 * kw + vw]
    g = blk_ref[:, 2 * kw + vw:2 * kw + 2 * vw].astype(F32)
    q_in = q * jnp.exp(cum)
    k_in = (k * jnp.exp(-cum)).astype(BF16)
    k_st = (k * jnp.exp(total - cum)).astype(BF16)
    lane = lax.broadcasted_iota(jnp.int32, (1, kw), 1)
    in_head = [(lane >= h * GLA_K_DIM) & (lane < (h + 1) * GLA_K_DIM) for h in range(GLA_HEADS)]

    state = state_ref[...]
    o_inter = []
    for j in range(n_chunks):
        sl = slice(j * c, (j + 1) * c)
        q_stack = jnp.concatenate([jnp.where(m, q_in[sl], 0.0) for m in in_head], axis=0).astype(BF16)
        k_stack = jnp.concatenate([jnp.where(m, k_st[sl], 0) for m in in_head], axis=0)
        v_stack = jnp.concatenate([v[sl, h * GLA_V_DIM:(h + 1) * GLA_V_DIM] for h in range(GLA_HEADS)], axis=0)
        oi = _dot(q_stack, state.astype(BF16))
        o_inter.append(jnp.concatenate([oi[h * c:(h + 1) * c] for h in range(GLA_HEADS)], axis=1))
        last = cum[(j + 1) * c - 1:(j + 1) * c, :]
        decay = jnp.exp(jnp.broadcast_to(last, (LANES, kw)).T)
        state = state * decay + _dot_tn(k_stack, v_stack)
    state_ref[...] = state
    o_inter = jnp.concatenate(o_inter, axis=0)
    outs = []
    for h in range(GLA_HEADS):
        qh = jnp.where(in_head[h], q_in, 0.0).astype(BF16)
        sc = jnp.where(tri, _dot_nt(qh, k_in), 0.0).astype(BF16)
        oh = _dot(sc, v[:, h * GLA_V_DIM:(h + 1) * GLA_V_DIM]) + o_inter[:, h * GLA_V_DIM:(h + 1) * GLA_V_DIM]
        outs.append(_rms(oh, ng_ref[:, h * GLA_V_DIM:(h + 1) * GLA_V_DIM]))
    y_ref[...] = (jnp.concatenate(outs, axis=1) * _silu(g)).astype(BF16)


def _gla(o_gla, o_small, w_gk2, b_gk, norm_g, b, l):
    rows = min(GLA_STEP, l)
    ns = l // rows
    kw = GLA_HEADS * GLA_K_DIM
    vw = GLA_HEADS * GLA_V_DIM
    w2 = jnp.zeros((LANES, kw), F32).at[SMALL_GGK_LANE:SMALL_GGK_LANE + GLA_GATE_RANK].set(w_gk2)
    w2_hi = w2.astype(BF16)
    w2_mid = (w2 - w2_hi.astype(F32)).astype(BF16)
    w2 = jnp.concatenate([w2_hi, w2_mid, w2_hi], axis=0)
    streams, (blk_spec, sm_spec, out_spec) = _stream_specs(b, l, rows, (2 * kw + 2 * vw, LANES, vw))
    return pl.pallas_call(
        _per_stream(_gla_body, 2, 3, 1),
        out_shape=jax.ShapeDtypeStruct((b, l, vw), BF16),
        grid=(b // streams, ns),
        in_specs=[blk_spec, sm_spec,
                  _const_spec(w2.shape), _const_spec((1, kw)), _const_spec((1, vw))],
        out_specs=out_spec,
        scratch_shapes=[pltpu.VMEM((streams, kw, GLA_V_DIM), F32)],
        compiler_params=_params("arbitrary", "arbitrary"),
        name="gla_mixer",
    )(o_gla.reshape(b, l, -1), o_small.reshape(b, l, -1), w2, b_gk.reshape(1, -1),
      norm_g.reshape(1, -1)).reshape(b * l, vw)


def _mla_proj_kernel(blk_ref, c_ref, s_ref, qn_ref, kvn_ref, wq_ref, wqr_ref, wk_ref, wvt_ref,
                     q_ref, k_ref, vt_ref):
    o_ckv = MLA_Q_LORA
    o_kr = o_ckv + MLA_KV_LORA
    bk = vt_ref.shape[-1]
    lane = lax.broadcasted_iota(jnp.int32, (1, LANES), 1)
    half = MLA_ROPE // 2
    scale = (MLA_NOPE + MLA_ROPE) ** -0.5 * LOG2_E
    ones = jnp.ones((VT_ROWS - MLA_V, bk), BF16)
    for j in range(blk_ref.shape[0] // bk):
        sl = slice(j * bk, (j + 1) * bk)
        cq = _rms(blk_ref[sl, 0:MLA_Q_LORA].astype(F32), qn_ref[...]).astype(BF16)
        ckv = _rms(blk_ref[sl, o_ckv:o_kr].astype(F32), kvn_ref[...]).astype(BF16)
        c = c_ref[sl, :]
        s = s_ref[sl, :]
        ct = jnp.concatenate([c] * MLA_HEADS, axis=1)
        st = jnp.concatenate([s] * MLA_HEADS, axis=1)
        q = (_dot(cq, wq_ref[...]) * ct + _dot(cq, wqr_ref[...]) * st) * scale
        q_ref[sl, :] = q.astype(BF16)
        kr = blk_ref[sl, o_kr:o_kr + LANES].astype(F32)
        partner = jnp.where(lane < MLA_PE_LANE + half, pltpu.roll(kr, LANES - half, axis=1), pltpu.roll(kr, half, axis=1))
        kpe = kr * c + partner * s
        k = _dot(ckv, wk_ref[...]) + jnp.concatenate([kpe] * MLA_HEADS, axis=1)
        k_ref[sl, :] = k.astype(BF16)
        vt = _dot_nt(wvt_ref[...], ckv).astype(BF16)
        for h in range(MLA_HEADS):
            vt_ref[j, h, 0:MLA_V, :] = vt[h * MLA_V:(h + 1) * MLA_V, :]
            vt_ref[j, h, MLA_V:VT_ROWS, :] = ones


def _mla_proj(o_mla, c_tab, s_tab, q_norm, kv_norm, wq, wqr, wk, wvt, b, l):
    bk = min(ATTN_BLOCK_K, l)
    nk = l // bk
    per_step = MLA_PROJ_BLOCKS if nk % MLA_PROJ_BLOCKS == 0 else 1
    tm = per_step * bk
    ns = nk // per_step
    hw = MLA_HEADS * MLA_HEAD_PAD
    row = lambda bi, j: (bi * ns + j, 0)
    return pl.pallas_call(
        _mla_proj_kernel,
        out_shape=(jax.ShapeDtypeStruct((b * l, hw), BF16), jax.ShapeDtypeStruct((b * l, hw), BF16),
                   jax.ShapeDtypeStruct((b, nk, MLA_HEADS, VT_ROWS, bk), BF16)),
        grid=(b, ns),
        in_specs=[pl.BlockSpec((tm, o_mla.shape[1]), row), pl.BlockSpec((tm, LANES), row),
                  pl.BlockSpec((tm, LANES), row),
                  _const_spec((1, MLA_Q_LORA)), _const_spec((1, MLA_KV_LORA)),
                  _const_spec(wq.shape), _const_spec(wqr.shape), _const_spec(wk.shape), _const_spec(wvt.shape)],
        out_specs=(pl.BlockSpec((tm, hw), row), pl.BlockSpec((tm, hw), row),
                   pl.BlockSpec((None, per_step, MLA_HEADS, VT_ROWS, bk), lambda bi, j: (bi, j, 0, 0, 0))),
        compiler_params=_params("arbitrary", "arbitrary"),
        name="mla_proj",
    )(o_mla, c_tab, s_tab, q_norm.reshape(1, -1), kv_norm.reshape(1, -1), wq, wqr, wk, wvt)


def _flash_kernel(nq, tq_ref, tk_ref, q_ref, k_ref, vt_ref, o_ref, m_ref, acc_ref, s_ref, mb0_ref, mb1_ref):
    bq, bk = ATTN_BLOCK_Q, ATTN_BLOCK_K
    half = bq // 2
    n_off = nq * (nq - 1) // 2
    n_blk = n_off + nq
    m_ref[...] = jnp.full(m_ref.shape, -jnp.inf, F32)
    acc_ref[...] = jnp.zeros(acc_ref.shape, F32)
    mb_refs = (mb0_ref, mb1_ref)

    hk = bk // 2
    lanes = lambda slot: slice(slot * half, (slot + 1) * half)

    def causal(s):
        kpos = lax.broadcasted_iota(jnp.int32, s.shape, 0)
        qpos = lax.broadcasted_iota(jnp.int32, s.shape, 1)
        return jnp.where(kpos <= qpos, s, -jnp.inf)

    def produce(t, slot, diagonal):
        q = q_ref[pl.ds(pl.multiple_of(tq_ref[t] * bq, bq), bq), :]
        k = k_ref[pl.ds(pl.multiple_of(tk_ref[t] * bk, bk), bk), :]
        if diagonal:
            top = causal(_dot_nt(k[0:hk], q))
            low = causal(_dot_nt(k[hk:bk], q[half:bq]))
            s_ref[0:hk, lanes(slot)] = top[:, 0:half]
            s_ref[bk:bk + hk, lanes(slot)] = top[:, half:bq]
            s_ref[bk + hk:2 * bk, lanes(slot)] = low
            m_top = jnp.max(top, axis=0, keepdims=True)
            mb_refs[slot][...] = jnp.concatenate(
                [m_top[:, 0:half], jnp.maximum(m_top[:, half:bq], jnp.max(low, axis=0, keepdims=True))], axis=1)
        else:
            s = _dot_nt(k, q)
            s_ref[0:bk, lanes(slot)] = s[:, 0:half]
            s_ref[bk:2 * bk, lanes(slot)] = s[:, half:bq]
            mb_refs[slot][...] = jnp.max(s, axis=0, keepdims=True)

    def consume(t, slot, diagonal):
        qi = tq_ref[t]
        m_old = m_ref[qi]
        m_new = jnp.maximum(m_old, mb_refs[slot][...])
        alpha = jnp.exp2(m_old - m_new)
        vt = vt_ref[tk_ref[t]]
        if diagonal:
            p_early = jnp.exp2(s_ref[0:hk, lanes(slot)] - m_new[:, 0:half]).astype(BF16)
            p_late = jnp.exp2(s_ref[bk:2 * bk, lanes(slot)] - m_new[:, half:bq]).astype(BF16)
            pv = jnp.concatenate([_dot(vt[:, 0:hk], p_early), _dot(vt, p_late)], axis=1)
            acc = acc_ref[qi] * alpha + pv
            o_ref[qi] = (acc[0:MLA_V, :] / acc[MLA_V:MLA_V + 1, :]).astype(BF16)
        else:
            s = jnp.concatenate([s_ref[0:bk, lanes(slot)], s_ref[bk:2 * bk, lanes(slot)]], axis=1)
            p = jnp.exp2(s - m_new).astype(BF16)
            acc_ref[qi] = acc_ref[qi] * alpha + _dot(vt, p)
            m_ref[qi] = m_new

    def static_steps(t_from, t_to):
        for t in range(t_from, t_to):
            if t + 1 < n_blk:
                produce(t + 1, (t + 1) % 2, t + 1 >= n_off)
            consume(t, t % 2, t >= n_off)

    u = ATTN_UNROLL

    def group_loop(t_from, n_groups, diagonal):
        def group(i, carry):
            t0 = t_from + u * i
            for d in range(u):
                produce(t0 + d + 1, (d + 1) % 2, diagonal)
                consume(t0 + d, d % 2, diagonal)
            return carry

        if n_groups > 0:
            assert t_from % 2 == 0 and u % 2 == 0
            lax.fori_loop(0, n_groups, group, 0)

    produce(0, 0, n_off == 0)
    n_a = max(0, (n_off - 1) // u)
    group_loop(0, n_a, False)
    t_b = min(n_off + n_off % 2, n_blk)
    static_steps(u * n_a, t_b)
    n_b = max(0, (n_blk - 1 - t_b) // u)
    group_loop(t_b, n_b, True)
    static_steps(t_b + u * n_b, n_blk)


def _flash_order(nq):
    off = [(qi, kb) for qi in range(nq) for kb in range(qi)]
    blocks = off + [(qi, qi) for qi in range(nq)]
    return (jnp.asarray([b[0] for b in blocks], jnp.int32), jnp.asarray([b[1] for b in blocks], jnp.int32))


def _flash(q, k, vt, b, l):
    bq, bk = ATTN_BLOCK_Q, ATTN_BLOCK_K
    assert bq == bk
    nq = l // bq
    nk = l // bk
    q3 = q.reshape(b, l, MLA_HEADS * MLA_HEAD_PAD)
    k3 = k.reshape(b, l, MLA_HEADS * MLA_HEAD_PAD)
    tq, tk = _flash_order(nq)
    grid_spec = pltpu.PrefetchScalarGridSpec(
        num_scalar_prefetch=2,
        grid=(b, MLA_HEADS),
        in_specs=[pl.BlockSpec((None, l, MLA_HEAD_PAD), lambda bi, h, *_: (bi, 0, h)),
                  pl.BlockSpec((None, l, MLA_HEAD_PAD), lambda bi, h, *_: (bi, 0, h)),
                  pl.BlockSpec((None, nk, None, VT_ROWS, bk), lambda bi, h, *_: (bi, 0, h, 0, 0))],
        out_specs=pl.BlockSpec((None, None, nq, MLA_V, bq), lambda bi, h, *_: (bi, h, 0, 0, 0)),
        scratch_shapes=[pltpu.VMEM((nq, 1, bq), F32), pltpu.VMEM((nq, VT_ROWS, bq), F32),
                        pltpu.VMEM((2 * bk, bq), F32),
                        pltpu.VMEM((1, bq), F32), pltpu.VMEM((1, bq), F32)])
    return pl.pallas_call(
        functools.partial(_flash_kernel, nq),
        out_shape=jax.ShapeDtypeStruct((b, MLA_HEADS, nq, MLA_V, bq), BF16),
        grid_spec=grid_spec,
        compiler_params=_params("arbitrary", "arbitrary"),
        name="mla_flash",
    )(tq, tk, q3, k3, vt)


def _merge_kernel(x_ref, yssd_ref, yret_ref, ymt_ref, ygla_ref, gpre_ref, wg_ref, bg_ref, wb_ref, wo_ref,
                  gpost_ref, o_ref):
    x = x_ref[...]
    h = _rms(x, gpre_ref[...]).astype(BF16)
    branches = (_dot(yssd_ref[...], wb_ref[0]), _dot(yret_ref[...], wb_ref[1]),
                _dot_tn(ymt_ref[...].reshape(BRANCH_WIDTH, -1), wb_ref[2]), _dot(ygla_ref[...], wb_ref[3]))
    merged = None
    for i, br in enumerate(branches):
        gate = jax.nn.sigmoid(_dot(h, wg_ref[:, i * D_MODEL:(i + 1) * D_MODEL]) + bg_ref[:, i * D_MODEL:(i + 1) * D_MODEL])
        merged = gate * br if merged is None else merged + gate * br
    o = _dot(merged.astype(BF16), wo_ref[...])
    o_ref[...] = x + _rms(o, gpost_ref[...])


def _merge(x3, y_ssd, y_ret, y_mla_t, y_gla, g_pre, wg, b_gate, wb, wo, g_post):
    b, l, _ = x3.shape
    tm = min(TOKEN_TILE, l)
    assert tm == y_mla_t.shape[-1]
    nt = l // tm
    w = BRANCH_WIDTH
    tok = lambda width: pl.BlockSpec((None, tm, width), lambda bi, j: (bi, j, 0))
    return pl.pallas_call(
        _merge_kernel,
        out_shape=jax.ShapeDtypeStruct((b, l, D_MODEL), F32),
        grid=(b, nt),
        in_specs=[tok(D_MODEL), tok(w), tok(w),
                  pl.BlockSpec((None, MLA_HEADS, None, MLA_V, tm), lambda bi, j: (bi, 0, j, 0, 0)), tok(w),
                  _const_spec((1, D_MODEL)), _const_spec(wg.shape), _const_spec((1, N_BRANCHES * D_MODEL)),
                  _const_spec(wb.shape), _const_spec(wo.shape), _const_spec((1, D_MODEL))],
        out_specs=tok(D_MODEL),
        compiler_params=_params("arbitrary", "arbitrary"),
        name="merge_out",
    )(x3, y_ssd.reshape(b, l, w), y_ret.reshape(b, l, w), y_mla_t, y_gla.reshape(b, l, w),
      g_pre, wg, b_gate.reshape(1, -1), wb, wo, g_post)


def _mlp_kernel(x_ref, gpre_ref, wi_ref, wo_ref, gpost_ref, o_ref):
    x = x_ref[...]
    h = _rms(x, gpre_ref[...]).astype(BF16)
    acc = None
    for j in range(D_FF // D_MODEL):
        u = jnp.maximum(_dot(h, wi_ref[:, j * D_MODEL:(j + 1) * D_MODEL]), 0.0)
        part = _dot((u * u).astype(BF16), wo_ref[j * D_MODEL:(j + 1) * D_MODEL, :])
        acc = part if acc is None else acc + part
    o_ref[...] = x + _rms(acc, gpost_ref[...])


def _mlp(x2, g_pre, wi, wo, g_post):
    t = x2.shape[0]
    tm = min(WIDE_TILE, t)
    return pl.pallas_call(
        _mlp_kernel,
        out_shape=jax.ShapeDtypeStruct((t, D_MODEL), F32),
        grid=(t // tm,),
        in_specs=[pl.BlockSpec((tm, D_MODEL), lambda i: (i, 0)), _const_spec((1, D_MODEL)),
                  _const_spec(wi.shape), _const_spec(wo.shape), _const_spec((1, D_MODEL))],
        out_specs=pl.BlockSpec((tm, D_MODEL), lambda i: (i, 0)),
        compiler_params=_params("arbitrary"),
        name="mlp",
    )(x2, g_pre, wi, wo, g_post)


def _rot_perm(n_heads, dim):
    half = dim // 2
    return np.concatenate([h * dim + (np.arange(dim) + half) % dim for h in range(n_heads)])


def _layer_weights(w_in, mla_w_uq, mla_w_ukv):
    col = lambda a, b: w_in[:, a:b]
    w_ssd = col(_O_Z, _O_DT).astype(BF16)
    w_ret = col(_O_RQ, _O_CQ).astype(BF16)
    zl = jnp.zeros((D_MODEL, MLA_PE_LANE), F32)
    zr = jnp.zeros((D_MODEL, LANES - MLA_PE_LANE - MLA_ROPE), F32)
    kperm = _rot_perm(1, MLA_ROPE)
    w_mla = jnp.concatenate([col(_O_CQ, _O_KR), zl, col(_O_KR, _O_GQ), zr], axis=1).astype(BF16)
    w_gla = jnp.concatenate([col(_O_GQ, _O_GGK), col(_O_GG, _O_GATE)], axis=1).astype(BF16)
    w_small = jnp.concatenate(
        [col(_O_DT, _O_RQ), col(_O_GGK, _O_GG),
         jnp.zeros((D_MODEL, LANES - SSD_HEADS - GLA_GATE_RANK), F32)], axis=1).astype(BF16)
    w_gate = col(_O_GATE, _O_END).astype(BF16)

    hd = MLA_NOPE + MLA_ROPE
    uq = mla_w_uq.reshape(MLA_Q_LORA, MLA_HEADS, hd)
    padq = jnp.zeros((MLA_Q_LORA, MLA_HEADS, MLA_HEAD_PAD - hd), F32)
    wq = jnp.concatenate([uq, padq], axis=2).reshape(MLA_Q_LORA, -1).astype(BF16)
    uq_rot = uq[:, :, MLA_NOPE + kperm]
    wqr = jnp.concatenate([jnp.zeros((MLA_Q_LORA, MLA_HEADS, MLA_NOPE), F32), uq_rot, padq],
                          axis=2).reshape(MLA_Q_LORA, -1).astype(BF16)
    ukv = mla_w_ukv.reshape(MLA_KV_LORA, MLA_HEADS, MLA_NOPE + MLA_V)
    wk = jnp.concatenate([ukv[:, :, :MLA_NOPE],
                          jnp.zeros((MLA_KV_LORA, MLA_HEADS, MLA_HEAD_PAD - MLA_NOPE), F32)],
                         axis=2).reshape(MLA_KV_LORA, -1).astype(BF16)
    wvt = ukv[:, :, MLA_NOPE:].reshape(MLA_KV_LORA, -1).T.astype(BF16)
    return w_ssd, w_ret, w_mla, w_gla, w_small, w_gate, wq, wqr, wk, wvt


def _layer(x3, tabs, proj, b_gate, ssd_conv_w, ssd_conv_b, ssd_dt_bias, ssd_a_log, ssd_d, ssd_norm, ret_norm,
           mla_q_norm, mla_kv_norm, gla_w_gk2, gla_b_gk, gla_norm, w_branch, w_out,
           norm_pre_mix, norm_post_mix, norm_pre_mlp, norm_post_mlp, w_mlp_in, w_mlp_out):
    b, l, d = x3.shape
    c_ret, s_ret, c_mla, s_mla = tabs
    w_ssd, w_ret, w_mla, w_gla, w_small, w_gate, wq, wqr, wk, wvt = proj
    g_pre = norm_pre_mix.reshape(1, -1)
    x2 = x3.reshape(b * l, d)
    o_ssd, o_ret, o_mla, o_gla, o_small = _inproj(x2, g_pre, w_ssd, w_ret, w_mla, w_gla, w_small)
    y_ssd = _ssd(o_ssd, o_small, ssd_conv_w, ssd_conv_b, ssd_dt_bias, ssd_a_log, ssd_d, ssd_norm, b, l)
    y_ret = _ret(o_ret, c_ret, s_ret, ret_norm, b, l)
    q, k, vt = _mla_proj(o_mla, c_mla, s_mla, mla_q_norm, mla_kv_norm, wq, wqr, wk, wvt, b, l)
    y_mla_t = _flash(q, k, vt, b, l)
    y_gla = _gla(o_gla, o_small, gla_w_gk2, gla_b_gk, gla_norm, b, l)
    x3 = _merge(x3, y_ssd, y_ret, y_mla_t, y_gla, g_pre, w_gate, b_gate, w_branch, w_out,
                norm_post_mix.reshape(1, -1))
    x2 = _mlp(x3.reshape(b * l, d), norm_pre_mlp.reshape(1, -1), w_mlp_in, w_mlp_out, norm_post_mlp.reshape(1, -1))
    return x2.reshape(b, l, d)


def kernel(x, positions, w_in, b_gate, ssd_conv_w, ssd_conv_b, ssd_dt_bias, ssd_a_log, ssd_d, ssd_norm, ret_norm, mla_q_norm, mla_w_uq, mla_kv_norm, mla_w_ukv, gla_w_gk2, gla_b_gk, gla_norm, w_branch, w_out, norm_pre_mix, norm_post_mix, norm_pre_mlp, norm_post_mlp, w_mlp_in, w_mlp_out):
    per_layer = (b_gate, ssd_conv_w, ssd_conv_b, ssd_dt_bias, ssd_a_log, ssd_d, ssd_norm, ret_norm,
                 mla_q_norm, mla_kv_norm, gla_w_gk2, gla_b_gk, gla_norm, w_branch.astype(BF16), w_out.astype(BF16),
                 norm_pre_mix, norm_post_mix, norm_pre_mlp, norm_post_mlp, w_mlp_in.astype(BF16),
                 w_mlp_out.astype(BF16))
    proj = jax.vmap(_layer_weights)(w_in, mla_w_uq, mla_w_ukv)
    tabs = _rope_tables(positions)
    for i in range(w_in.shape[0]):
        x = _layer(x, tabs, tuple(p[i] for p in proj), *(p[i] for p in per_layer))
    return x
```
